```python
import math
import jax
import jax.numpy as jnp
from jax import lax
import numpy as np

D_MODEL = 4096
BATCH = 1
SEQ = 8192
DEPTH = 4
DEC_BATCH = 16
DEC_SEQ = 32
PAST_LEN = 1024

CHUNK = 64
N_ATT = (DEPTH + 1) // 2
N_REC = DEPTH // 2
MLA_HEADS = 16
Q_RANK = 1024
KV_RANK = 512
NOPE_DIM = 128
ROPE_DIM = 64
MLA_VDIM = 128
ROPE_BASE = 10000.0
BAND_HEADS = 16
BAND_HDIM = 128
BAND_PAST_CHUNKS = 8
REL_MAX = 256
N_REL = REL_MAX + CHUNK
S5_WIDTH = 2048
S5_GROUP = 16
S5_GROUPS = S5_WIDTH // S5_GROUP
S5_STATE = 64
DT_MIN = 1e-3
DT_MAX = 1e-1
HG_HEADS = 16
HG_KDIM = 128
HG_VDIM = 128
N_MEM = 256
MEM_HEADS = 4
MEM_HDIM = 128
MEM_WIDTH = MEM_HEADS * MEM_HDIM
N_EXPERTS = 32
TOP_K = 4
D_EXPERT = 1024
SWIGLU_LIMIT = 7.0
SWIGLU_ALPHA = 1.702
Q_BLOCK = 128
EPS = 1e-6
NEG_INF = -1e30
ATT_IN = Q_RANK + KV_RANK + ROPE_DIM + 3 * BAND_HEADS * BAND_HDIM
ATT_OUT = MLA_HEADS * MLA_VDIM + BAND_HEADS * BAND_HDIM
REC_IN = S5_WIDTH + 2 * HG_HEADS * HG_KDIM + 2 * HG_HEADS * HG_VDIM
REC_OUT = S5_WIDTH + HG_HEADS * HG_VDIM

kernel_name = 'hybrid_chunk_streaming_encoder_step'

F32 = jnp.float32


def _rmsnorm(x, g):
    x32 = x.astype(F32)
    y = x32 * lax.rsqrt(jnp.mean(x32 * x32, axis=-1, keepdims=True) + EPS)
    return (y * g.astype(F32)).astype(x.dtype)


def _rope(x, pos):
    half = x.shape[-1] // 2
    freqs = jnp.exp(-math.log(ROPE_BASE) * jnp.arange(half, dtype=F32) / half)
    ang = pos.astype(F32)[:, None] * freqs[None, :]
    ang = ang.reshape((1, pos.shape[0]) + (1,) * (x.ndim - 3) + (half,))
    cos, sin = jnp.cos(ang), jnp.sin(ang)
    x32 = x.astype(F32)
    x1, x2 = x32[..., :half], x32[..., half:]
    return jnp.concatenate([x1 * cos - x2 * sin, x2 * cos + x1 * sin], axis=-1).astype(x.dtype)


def _attend(q, k, v, q_pos, k_pos, band_chunks=None, bias=None):
    s = jnp.einsum('bqhd,bkhd->bhqk', q, k, preferred_element_type=F32) * (q.shape[-1] ** -0.5)
    if bias is not None:
        s = s + bias[None]
    qc = (q_pos // CHUNK)[:, None]
    kc = (k_pos // CHUNK)[None, :]
    ok = (kc <= qc) & (k_pos >= 0)[None, :]
    if band_chunks is not None:
        ok = ok & (kc >= qc - band_chunks)
    s = jnp.where(ok[None, None], s, NEG_INF)
    p = jax.nn.softmax(s, axis=-1)
    return jnp.einsum('bhqk,bkhd->bqhd', p.astype(v.dtype), v)


def _rel_bias(table, q_pos, k_pos):
    rel = jnp.clip(q_pos[:, None] - k_pos[None, :], -(CHUNK - 1), REL_MAX) + (CHUNK - 1)
    return jnp.take(table.astype(F32), rel, axis=1)


def _dense_chunk_attn(q, k, v, q_pos, k_pos):
    b, lq, h, dk = q.shape
    qb = min(Q_BLOCK, lq)
    nb = lq // qb
    if nb == 1:
        return _attend(q, k, v, q_pos, k_pos)
    qs = jnp.swapaxes(q.reshape(b, nb, qb, h, dk), 0, 1)
    ps = q_pos.reshape(nb, qb)
    out = lax.map(lambda a: _attend(a[0], k, v, a[1], k_pos), (qs, ps))
    return jnp.swapaxes(out, 0, 1).reshape(b, lq, h, v.shape[-1])


def _band_attn_prompt(q, k, v, table):
    b, l, h, d = q.shape
    pad = BAND_PAST_CHUNKS * CHUNK
    span = pad + CHUNK
    nc = l // CHUNK
    kp = jnp.pad(k, ((0, 0), (pad, 0), (0, 0), (0, 0)))
    vp = jnp.pad(v, ((0, 0), (pad, 0), (0, 0), (0, 0)))
    kpos = jnp.arange(-pad, l)
    qs = jnp.swapaxes(q.reshape(b, nc, CHUNK, h, d), 0, 1)

    def one(a):
        qc, c = a
        start = c * CHUNK
        kb = lax.dynamic_slice_in_dim(kp, start, span, axis=1)
        vb = lax.dynamic_slice_in_dim(vp, start, span, axis=1)
        kpb = lax.dynamic_slice_in_dim(kpos, start, span)
        qpos = start + jnp.arange(CHUNK)
        return _attend(qc, kb, vb, qpos, kpb, BAND_PAST_CHUNKS, _rel_bias(table, qpos, kpb))

    out = lax.map(one, (qs, jnp.arange(nc)))
    return jnp.swapaxes(out, 0, 1).reshape(b, l, h, d)


def _att_mixer(h, pos, cache, p, j):
    b, l, _ = h.shape
    bw = BAND_HEADS * BAND_HDIM
    o1 = Q_RANK
    o2 = o1 + KV_RANK
    o3 = o2 + ROPE_DIM
    o4 = o3 + bw
    o5 = o4 + bw
    z = h @ p['w_in_att'][j]
    cq, ckv, kr = z[..., :o1], z[..., o1:o2], z[..., o2:o3]
    qb = z[..., o3:o4].reshape(b, l, BAND_HEADS, BAND_HDIM)
    kb = z[..., o4:o5].reshape(b, l, BAND_HEADS, BAND_HDIM)
    vb = z[..., o5:].reshape(b, l, BAND_HEADS, BAND_HDIM)
    q = (_rmsnorm(cq, p['g_q'][j]) @ p['w_uq'][j]).reshape(b, l, MLA_HEADS, NOPE_DIM + ROPE_DIM)
    q = jnp.concatenate([q[..., :NOPE_DIM], _rope(q[..., NOPE_DIM:], pos)], axis=-1)
    lat = _rmsnorm(ckv, p['g_kv'][j])
    kr = _rope(kr, pos)
    if cache is None:
        lat_all, kr_all, kpos = lat, kr, pos
    else:
        c_lat, c_kr, c_bk, c_bv = cache
        past = c_lat.shape[1]
        lat_all = jnp.concatenate([c_lat.astype(lat.dtype), lat], axis=1)
        kr_all = jnp.concatenate([c_kr.astype(kr.dtype), kr], axis=1)
        kpos = jnp.arange(past + l)
    nk = lat_all.shape[1]
    kv = (lat_all @ p['w_ukv'][j]).reshape(b, nk, MLA_HEADS, NOPE_DIM + MLA_VDIM)
    k = jnp.concatenate(
        [kv[..., :NOPE_DIM], jnp.broadcast_to(kr_all[:, :, None, :], (b, nk, MLA_HEADS, ROPE_DIM))], axis=-1)
    o_a = _dense_chunk_attn(q, k, kv[..., NOPE_DIM:], pos, kpos)
    if cache is None:
        o_b = _band_attn_prompt(qb, kb, vb, p['rel_bias'][j])
        keep = min(BAND_PAST_CHUNKS * CHUNK, l)
        bk_new, bv_new = kb[:, l - keep:], vb[:, l - keep:]
    else:
        keep = c_bk.shape[1]
        kall = jnp.concatenate([c_bk.astype(kb.dtype), kb], axis=1)
        vall = jnp.concatenate([c_bv.astype(vb.dtype), vb], axis=1)
        bpos = jnp.arange(past - keep, past + l)
        o_b = _attend(qb, kall, vall, pos, bpos, BAND_PAST_CHUNKS, _rel_bias(p['rel_bias'][j], pos, bpos))
        bk_new, bv_new = kb, vb
    o = jnp.concatenate([o_a.reshape(b, l, -1), o_b.reshape(b, l, -1)], axis=-1)
    return o @ p['w_out_att'][j], lat, kr, bk_new, bv_new


def _s5_scan(u, h0_re, h0_im, lam_re, lam_im, log_dt, b_re, b_im, c_re, c_im, d_skip):
    dt = jnp.exp(log_dt.astype(F32))[:, None]
    lr, li = lam_re.astype(F32), lam_im.astype(F32)
    mag = jnp.exp(lr * dt)
    ab_re, ab_im = mag * jnp.cos(li * dt), mag * jnp.sin(li * dt)
    nr, ni = ab_re - 1.0, ab_im
    den = lr * lr + li * li
    cf_re = (nr * lr + ni * li) / den
    cf_im = (ni * lr - nr * li) / den
    br, bi = b_re.astype(F32), b_im.astype(F32)
    bb_re = cf_re[..., None] * br - cf_im[..., None] * bi
    bb_im = cf_re[..., None] * bi + cf_im[..., None] * br
    bu_re = jnp.einsum('blgs,gps->blgp', u, bb_re)
    bu_im = jnp.einsum('blgs,gps->blgp', u, bb_im)
    h0r, h0i = h0_re.astype(F32), h0_im.astype(F32)
    bu_re = bu_re.at[:, 0].add(ab_re * h0r - ab_im * h0i)
    bu_im = bu_im.at[:, 0].add(ab_re * h0i + ab_im * h0r)
    a_re = jnp.broadcast_to(ab_re, bu_re.shape)
    a_im = jnp.broadcast_to(ab_im, bu_im.shape)

    def combine(e1, e2):
        a1r, a1i, b1r, b1i = e1
        a2r, a2i, b2r, b2i = e2
        return (a2r * a1r - a2i * a1i, a2r * a1i + a2i * a1r,
                a2r * b1r - a2i * b1i + b2r, a2r * b1i + a2i * b1r + b2i)

    _, _, hr, hi = lax.associative_scan(combine, (a_re, a_im, bu_re, bu_im), axis=1)
    y = (jnp.einsum('gsp,blgp->blgs', c_re.astype(F32), hr)
         - jnp.einsum('gsp,blgp->blgs', c_im.astype(F32), hi))
    y = y + d_skip.astype(F32).reshape(S5_GROUPS, S5_GROUP) * u
    return y, hr[:, -1], hi[:, -1]


def _hgrn2_chunkwise(q, logf, k, v, s0, blk):
    b, l, h, kd = q.shape
    vd = v.shape[-1]
    nc = l // blk

    def split(t):
        return jnp.swapaxes(t.reshape(b, nc, blk, h, t.shape[-1]), 0, 1)

    causal = jnp.tril(jnp.ones((blk, blk), dtype=bool))[None, :, :, None, None]

    def step(S, xs):
        qc, gc, kc, vc = xs
        G = jnp.cumsum(gc, axis=1)
        o_inter = jnp.einsum('bihk,bhkv->bihv', qc * jnp.exp(G), S)
        diff = G[:, :, None] - G[:, None, :]
        decay = jnp.where(causal, jnp.exp(jnp.where(causal, diff, 0.0)), 0.0)
        A = jnp.einsum('bihk,bjhk,bijhk->bhij', qc, kc, decay)
        o_intra = jnp.einsum('bhij,bjhv->bihv', A, vc)
        g_last = G[:, -1]
        k_dec = kc * jnp.exp(g_last[:, None] - G)
        S_new = jnp.exp(g_last)[..., None] * S + jnp.einsum('bjhk,bjhv->bhkv', k_dec, vc)
        return S_new, o_inter + o_intra

    s_fin, o = lax.scan(step, s0, (split(q), split(logf), split(k), split(v)))
    return jnp.swapaxes(o, 0, 1).reshape(b, l, h, vd), s_fin


def _rec_mixer(h, h0r, h0i, s0, lb, p, j):
    b, l, _ = h.shape
    kw = HG_HEADS * HG_KDIM
    vw = HG_HEADS * HG_VDIM
    o1 = S5_WIDTH
    o2 = o1 + kw
    o3 = o2 + kw
    o4 = o3 + vw
    z = h @ p['w_in_rec'][j]
    u = z[..., :o1].astype(F32).reshape(b, l, S5_GROUPS, S5_GROUP)
    y, hr, hi = _s5_scan(u, h0r, h0i, p['s5_lam_re'][j], p['s5_lam_im'][j], p['s5_log_dt'][j],
                         p['s5_b_re'][j], p['s5_b_im'][j], p['s5_c_re'][j], p['s5_c_im'][j], p['s5_d'][j])
    y = jax.nn.gelu(y.reshape(b, l, S5_WIDTH)).astype(h.dtype)
    o_c = y * jax.nn.sigmoid(y @ p['w_glu'][j])
    qd = jax.nn.silu(z[..., o1:o2].astype(F32)).reshape(b, l, HG_HEADS, HG_KDIM)
    fd = z[..., o2:o3].astype(F32).reshape(b, l, HG_HEADS, HG_KDIM)
    lbh = lb.reshape(HG_HEADS, HG_KDIM)
    logf = jnp.log(lbh + (1.0 - lbh) * jax.nn.sigmoid(fd))
    kd = (1.0 - lbh) * jax.nn.sigmoid(-fd)
    vd = z[..., o3:o4].astype(F32).reshape(b, l, HG_HEADS, HG_VDIM)
    gd = z[..., o4:].astype(F32).reshape(b, l, HG_HEADS, HG_VDIM)
    od, s_new = _hgrn2_chunkwise(qd, logf, kd, vd, s0.astype(F32), min(CHUNK, l))
    od = _rmsnorm(od, p['g_hg'][j]) * jax.nn.silu(gd)
    o = jnp.concatenate([o_c, od.reshape(b, l, vw).astype(h.dtype)], axis=-1)
    return o @ p['w_out_rec'][j], hr, hi, s_new


def _mem_kv(mem, p, l):
    b, n, _ = mem.shape
    kv = (_rmsnorm(mem, p['g_mem'][l]) @ p['w_mkv'][l]).reshape(b, n, 2, MEM_HEADS, MEM_HDIM)
    return kv[:, :, 0], kv[:, :, 1]


def _mem_xattn(h, mk, mv, p, l):
    b, lq, _ = h.shape
    q = (h @ p['w_mq'][l]).reshape(b, lq, MEM_HEADS, MEM_HDIM)
    s = jnp.einsum('bqhd,bkhd->bhqk', q, mk.astype(q.dtype), preferred_element_type=F32) * (MEM_HDIM ** -0.5)
    pr = jax.nn.softmax(s, axis=-1)
    o = jnp.einsum('bhqk,bkhd->bqhd', pr.astype(h.dtype), mv.astype(h.dtype)).reshape(b, lq, MEM_WIDTH)
    return o @ p['w_mo'][l]


def _moe(h, p, l):
    b, lq, d = h.shape
    t = h.reshape(b * lq, d)
    logits = jnp.einsum('td,de->te', t, p['w_router'][l], preferred_element_type=F32) + p['b_router'][l].astype(F32)
    vals, idx = lax.top_k(logits, TOP_K)
    w = jax.nn.softmax(vals, axis=-1)
    gate = jnp.einsum('tk,tke->te', w, jax.nn.one_hot(idx, N_EXPERTS, dtype=F32))

    def expert(acc, xs):
        w1, b1, w2, b2, g = xs
        hh = t @ w1 + b1
        xg = jnp.minimum(hh[:, :D_EXPERT], SWIGLU_LIMIT)
        xl = jnp.clip(hh[:, D_EXPERT:], -SWIGLU_LIMIT, SWIGLU_LIMIT)
        a = xg * jax.nn.sigmoid(SWIGLU_ALPHA * xg) * (xl + 1.0)
        out = a @ w2 + b2
        return acc + g[:, None] * out.astype(F32), None

    acc, _ = lax.scan(expert, jnp.zeros((b * lq, d), F32),
                      (p['w_e1'][l], p['b_e1'][l], p['w_e2'][l], p['b_e2'][l], gate.T))
    return acc.astype(h.dtype).reshape(b, lq, d)


def _trunk(x, pos, mem, cache, p):
    prompt = cache is None
    b = x.shape[0]
    probs = jax.nn.softmax(p['hg_lb_logits'].astype(F32), axis=0)
    lb_all = jnp.cumsum(probs, axis=0) - probs
    lat_l, kr_l, bk_l, bv_l, mk_l, mv_l, sr_l, si_l, hg_l = [], [], [], [], [], [], [], [], []
    for l in range(DEPTH):
        j = l // 2
        h = _rmsnorm(x, p['g_mix'][l])
        if l % 2 == 0:
            c = None if prompt else (cache['lat'][j], cache['kr'][j], cache['bk'][j], cache['bv'][j])
            out, lat, kr, bk, bv = _att_mixer(h, pos, c, p, j)
            lat_l.append(lat)
            kr_l.append(kr)
            bk_l.append(bk)
            bv_l.append(bv)
        else:
            if prompt:
                h0r = jnp.zeros((b, S5_GROUPS, S5_STATE), F32)
                h0i = jnp.zeros((b, S5_GROUPS, S5_STATE), F32)
                s0 = jnp.zeros((b, HG_HEADS, HG_KDIM, HG_VDIM), F32)
            else:
                h0r, h0i, s0 = cache['s5r'][j], cache['s5i'][j], cache['hg'][j]
            out, hr, hi, s_new = _rec_mixer(h, h0r, h0i, s0, lb_all[j], p, j)
            sr_l.append(hr.astype(x.dtype))
            si_l.append(hi.astype(x.dtype))
            hg_l.append(s_new.astype(x.dtype))
        x = x + out
        if prompt:
            mk, mv = _mem_kv(mem, p, l)
            mk_l.append(mk)
            mv_l.append(mv)
        else:
            mk, mv = cache['mk'][l], cache['mv'][l]
        x = x + _mem_xattn(_rmsnorm(x, p['g_xattn'][l]), mk, mv, p, l)
        x = x + _moe(_rmsnorm(x, p['g_moe'][l]), p, l)
    y = _rmsnorm(x, p['g_final'])
    mk_s = jnp.stack(mk_l, 0) if prompt else None
    mv_s = jnp.stack(mv_l, 0) if prompt else None
    return (y, jnp.stack(lat_l, 0), jnp.stack(kr_l, 0), jnp.stack(bk_l, 0), jnp.stack(bv_l, 0),
            jnp.stack(sr_l, 0), jnp.stack(si_l, 0), jnp.stack(hg_l, 0), mk_s, mv_s)


def setup_inputs(seed: int = 0) -> dict:
    key = jax.random.key(seed)
    keys = iter(jax.random.split(key, 64))

    def nrm(shape, scale=1.0):
        return scale * jax.random.normal(next(keys), shape, jnp.float32)

    def gain(shape):
        return 1.0 + 0.02 * jax.random.normal(next(keys), shape, jnp.float32)

    band_keep = min(BAND_PAST_CHUNKS * CHUNK, PAST_LEN)
    d_in = D_MODEL ** -0.5
    lam_im0 = jnp.pi * jnp.arange(S5_STATE, dtype=jnp.float32)
    return {
        'x_prompt': nrm((BATCH, SEQ, D_MODEL)),
        'x_sample': nrm((DEC_BATCH, DEC_SEQ, D_MODEL)),
        'cache_mla_latent': nrm((N_ATT, DEC_BATCH, PAST_LEN, KV_RANK)),
        'cache_mla_krope': nrm((N_ATT, DEC_BATCH, PAST_LEN, ROPE_DIM)),
        'cache_band_k': nrm((N_ATT, DEC_BATCH, band_keep, BAND_HEADS, BAND_HDIM)),
        'cache_band_v': nrm((N_ATT, DEC_BATCH, band_keep, BAND_HEADS, BAND_HDIM)),
        'cache_mem_k': nrm((DEPTH, DEC_BATCH, N_MEM, MEM_HEADS, MEM_HDIM)),
        'cache_mem_v': nrm((DEPTH, DEC_BATCH, N_MEM, MEM_HEADS, MEM_HDIM)),
        'state_s5_re': nrm((N_REC, DEC_BATCH, S5_GROUPS, S5_STATE), 0.3),
        'state_s5_im': nrm((N_REC, DEC_BATCH, S5_GROUPS, S5_STATE), 0.3),
        'state_hgrn': nrm((N_REC, DEC_BATCH, HG_HEADS, HG_KDIM, HG_VDIM), 0.5),
        'mem_prompt': nrm((BATCH, N_MEM, D_MODEL)),
        'g_mix': gain((DEPTH, D_MODEL)),
        'g_xattn': gain((DEPTH, D_MODEL)),
        'g_mem': gain((DEPTH, D_MODEL)),
        'g_moe': gain((DEPTH, D_MODEL)),
        'g_final': gain((D_MODEL,)),
        'w_in_att': nrm((N_ATT, D_MODEL, ATT_IN), d_in),
        'g_q': gain((N_ATT, Q_RANK)),
        'w_uq': nrm((N_ATT, Q_RANK, MLA_HEADS * (NOPE_DIM + ROPE_DIM)), Q_RANK ** -0.5),
        'g_kv': gain((N_ATT, KV_RANK)),
        'w_ukv': nrm((N_ATT, KV_RANK, MLA_HEADS * (NOPE_DIM + MLA_VDIM)), KV_RANK ** -0.5),
        'rel_bias': nrm((N_ATT, BAND_HEADS, N_REL), 0.1),
        'w_out_att': nrm((N_ATT, ATT_OUT, D_MODEL), 0.5 * ATT_OUT ** -0.5),
        'w_in_rec': nrm((N_REC, D_MODEL, REC_IN), d_in),
        's5_lam_re': -0.5 + nrm((N_REC, S5_GROUPS, S5_STATE), 0.01),
        's5_lam_im': lam_im0 + nrm((N_REC, S5_GROUPS, S5_STATE), 0.01),
        's5_log_dt': jax.random.uniform(next(keys), (N_REC, S5_GROUPS), jnp.float32,
                                        math.log(DT_MIN), math.log(DT_MAX)),
        's5_b_re': nrm((N_REC, S5_GROUPS, S5_STATE, S5_GROUP), (2 * S5_GROUP) ** -0.5),
        's5_b_im': nrm((N_REC, S5_GROUPS, S5_STATE, S5_GROUP), (2 * S5_GROUP) ** -0.5),
        's5_c_re': nrm((N_REC, S5_GROUPS, S5_GROUP, S5_STATE), S5_STATE ** -0.5),
        's5_c_im': nrm((N_REC, S5_GROUPS, S5_GROUP, S5_STATE), S5_STATE ** -0.5),
        's5_d': nrm((N_REC, S5_WIDTH)),
        'w_glu': nrm((N_REC, S5_WIDTH, S5_WIDTH), S5_WIDTH ** -0.5),
        'hg_lb_logits': nrm((N_REC, HG_HEADS * HG_KDIM), 0.5),
        'g_hg': gain((N_REC, HG_VDIM)),
        'w_out_rec': nrm((N_REC, REC_OUT, D_MODEL), 0.5 * REC_OUT ** -0.5),
        'w_mq': nrm((DEPTH, D_MODEL, MEM_WIDTH), d_in),
        'w_mkv': nrm((DEPTH, D_MODEL, 2 * MEM_WIDTH), d_in),
        'w_mo': nrm((DEPTH, MEM_WIDTH, D_MODEL), 0.5 * MEM_WIDTH ** -0.5),
        'w_router': nrm((DEPTH, D_MODEL, N_EXPERTS), d_in),
        'b_router': nrm((DEPTH, N_EXPERTS), 0.01),
        'w_e1': nrm((DEPTH, N_EXPERTS, D_MODEL, 2 * D_EXPERT), d_in),
        'b_e1': nrm((DEPTH, N_EXPERTS, 2 * D_EXPERT), 0.02),
        'w_e2': nrm((DEPTH, N_EXPERTS, D_EXPERT, D_MODEL), 0.5 * D_EXPERT ** -0.5),
        'b_e2': nrm((DEPTH, N_EXPERTS, D_MODEL), 0.02),
    }


def reference(x_prompt, x_sample, cache_mla_latent, cache_mla_krope, cache_band_k, cache_band_v,
              cache_mem_k, cache_mem_v, state_s5_re, state_s5_im, state_hgrn, mem_prompt,
              g_mix, g_xattn, g_mem, g_moe, g_final,
              w_in_att, g_q, w_uq, g_kv, w_ukv, rel_bias, w_out_att,
              w_in_rec, s5_lam_re, s5_lam_im, s5_log_dt, s5_b_re, s5_b_im, s5_c_re, s5_c_im, s5_d,
              w_glu, hg_lb_logits, g_hg, w_out_rec,
              w_mq, w_mkv, w_mo,
              w_router, b_router, w_e1, b_e1, w_e2, b_e2):
    p = dict(g_mix=g_mix, g_xattn=g_xattn, g_mem=g_mem, g_moe=g_moe, g_final=g_final,
             w_in_att=w_in_att, g_q=g_q, w_uq=w_uq, g_kv=g_kv, w_ukv=w_ukv, rel_bias=rel_bias,
             w_out_att=w_out_att, w_in_rec=w_in_rec, s5_lam_re=s5_lam_re, s5_lam_im=s5_lam_im,
             s5_log_dt=s5_log_dt, s5_b_re=s5_b_re, s5_b_im=s5_b_im, s5_c_re=s5_c_re, s5_c_im=s5_c_im,
             s5_d=s5_d, w_glu=w_glu, hg_lb_logits=hg_lb_logits, g_hg=g_hg, w_out_rec=w_out_rec,
             w_mq=w_mq, w_mkv=w_mkv, w_mo=w_mo, w_router=w_router, b_router=b_router,
             w_e1=w_e1, b_e1=b_e1, w_e2=w_e2, b_e2=b_e2)
    pos_p = jnp.arange(x_prompt.shape[1])
    (y_prompt, lat_p, kr_p, bk_p, bv_p, sr_p, si_p, hg_p, mk_p, mv_p) = _trunk(x_prompt, pos_p, mem_prompt, None, p)
    past = cache_mla_latent.shape[2]
    pos_s = past + jnp.arange(x_sample.shape[1])
    cache = dict(lat=cache_mla_latent, kr=cache_mla_krope, bk=cache_band_k, bv=cache_band_v,
                 mk=cache_mem_k, mv=cache_mem_v, s5r=state_s5_re, s5i=state_s5_im, hg=state_hgrn)
    (y_sample, lat_s, kr_s, bk_s, bv_s, sr_s, si_s, hg_s, _, _) = _trunk(x_sample, pos_s, None, cache, p)
    return (y_prompt, y_sample, lat_p, kr_p, bk_p, bv_p, mk_p, mv_p, sr_p, si_p, hg_p,
            lat_s, kr_s, bk_s, bv_s, sr_s, si_s, hg_s)
```

```python
import functools
import math

import numpy as np
import jax
import jax.numpy as jnp
from jax import lax
from jax.experimental import pallas as pl
from jax.experimental.pallas import tpu as pltpu

F32 = jnp.float32
BF16 = jnp.bfloat16
I32 = jnp.int32
U32 = jnp.uint32

CHUNK = 64
MLA_HEADS = 16
Q_RANK = 1024
KV_RANK = 512
NOPE_DIM = 128
ROPE_DIM = 64
MLA_VDIM = 128
ROPE_BASE = 10000.0
BAND_HEADS = 16
BAND_HDIM = 128
BAND_PAST_CHUNKS = 8
REL_MAX = 256
N_REL = REL_MAX + CHUNK
S5_GROUP = 16
S5_STATE = 64
HG_HEADS = 16
HG_KDIM = 128
HG_VDIM = 128
MEM_HEADS = 4
MEM_HDIM = 128
TOP_K = 4
SWIGLU_LIMIT = 7.0
SWIGLU_ALPHA = 1.702
EPS = 1e-6
NEG_INF = -1e30

LANES = 128
SUBLANES = 8
VMEM_LIMIT = 56 * 1024 * 1024
S5_SLAB_GROUPS = 8
MOE_TM = 512
BAND_TQ = 256


def _params(sem):
    return pltpu.CompilerParams(dimension_semantics=sem, vmem_limit_bytes=VMEM_LIMIT)


def _dot(a, b):
    return jnp.dot(a, b, preferred_element_type=F32)


def _dot_nt(a, b):
    return lax.dot_general(a, b, (((1,), (1,)), ((), ())), preferred_element_type=F32)


def _sigmoid(x):
    return 1.0 / (1.0 + jnp.exp(-x))


def _pick(n, pref):
    t = min(pref, n)
    while n % t:
        t //= 2
    return t


def _mm_kernel(*refs, has_gain, has_bias, has_res, has_glu):
    it = iter(refs)
    x_ref = next(it)
    w_ref = next(it)
    g_ref = next(it) if has_gain else None
    b_ref = next(it) if has_bias else None
    r_ref = next(it) if has_res else None
    y_ref = next(it) if has_glu else None
    o_ref = next(it)
    xs_ref = next(it)

    @pl.when(pl.program_id(1) == 0)
    def _():
        x = x_ref[...].astype(F32)
        if has_gain:
            ms = jnp.mean(x * x, axis=-1, keepdims=True)
            x = x * lax.rsqrt(ms + EPS) * g_ref[...]
        xs_ref[...] = x.astype(BF16)

    acc = _dot(xs_ref[...], w_ref[...])
    if has_bias:
        acc = acc + b_ref[...]
    if has_glu:
        acc = y_ref[...].astype(F32) * _sigmoid(acc)
    if has_res:
        acc = acc + r_ref[...]
    o_ref[...] = acc.astype(o_ref.dtype)


def _mm(x, w, *, gain=None, bias=None, residual=None, glu=None, xcol=0, rows=None, row0=0,
        tm=512, tn=512, out_dtype=F32, name="mm"):
    k, n = w.shape
    m = x.shape[0] if rows is None else rows
    tm = _pick(m, tm)
    tn = _pick(n, tn)
    assert row0 % tm == 0 and x.shape[1] % k == 0
    rb = row0 // tm
    in_specs = [pl.BlockSpec((tm, k), lambda i, j: (i + rb, xcol)),
                pl.BlockSpec((k, tn), lambda i, j: (0, j))]
    args = [x, w]
    if gain is not None:
        in_specs.append(pl.BlockSpec((1, k), lambda i, j: (0, 0)))
        args.append(gain.reshape(1, k).astype(F32))
    if bias is not None:
        in_specs.append(pl.BlockSpec((1, tn), lambda i, j: (0, j)))
        args.append(bias.reshape(1, n).astype(F32))
    if residual is not None:
        in_specs.append(pl.BlockSpec((tm, tn), lambda i, j: (i, j)))
        args.append(residual)
    if glu is not None:
        in_specs.append(pl.BlockSpec((tm, tn), lambda i, j: (i, j)))
        args.append(glu)
    kern = functools.partial(_mm_kernel, has_gain=gain is not None, has_bias=bias is not None,
                             has_res=residual is not None, has_glu=glu is not None)
    return pl.pallas_call(
        kern,
        grid=(m // tm, n // tn),
        in_specs=in_specs,
        out_specs=pl.BlockSpec((tm, tn), lambda i, j: (i, j)),
        out_shape=jax.ShapeDtypeStruct((m, n), out_dtype),
        scratch_shapes=[pltpu.VMEM((tm, k), BF16)],
        compiler_params=_params(("parallel", "arbitrary")),
        name=name,
    )(*args)


def _rms_kernel(x_ref, g_ref, o_ref):
    x = x_ref[...]
    ms = jnp.mean(x * x, axis=-1, keepdims=True)
    o_ref[...] = x * lax.rsqrt(ms + EPS) * g_ref[...]


def _rmsnorm_rows(x, g, tm=256):
    m, d = x.shape
    tm = _pick(m, tm)
    return pl.pallas_call(
        _rms_kernel,
        grid=(m // tm,),
        in_specs=[pl.BlockSpec((tm, d), lambda i: (i, 0)), pl.BlockSpec((1, d), lambda i: (0, 0))],
        out_specs=pl.BlockSpec((tm, d), lambda i: (i, 0)),
        out_shape=jax.ShapeDtypeStruct((m, d), F32),
        compiler_params=_params(("parallel",)),
        name="final_norm",
    )(x, g.reshape(1, d))


def _rope_table(pos):
    half = ROPE_DIM // 2
    freqs = jnp.exp(-math.log(ROPE_BASE) * jnp.arange(half, dtype=F32) / half)
    ang = pos.astype(F32)[:, None] * freqs[None, :]
    cos, sin = jnp.cos(ang), jnp.sin(ang)
    return jnp.concatenate([cos, cos, -sin, sin], axis=-1)


def _rope_block(blk, table):
    t = blk * table
    return t + pltpu.roll(t, ROPE_DIM, 1)


def _att_post_kernel(ckv_ref, kr_ref, g_ref, cs_ref, lat_ref, krr_ref, krp_ref):
    c = ckv_ref[...]
    ms = jnp.mean(c * c, axis=-1, keepdims=True)
    lat_ref[...] = c * lax.rsqrt(ms + EPS) * g_ref[...]
    rr = _rope_block(kr_ref[...], cs_ref[...])
    krr_ref[...] = rr
    lane = lax.broadcasted_iota(I32, rr.shape, 1)
    krp_ref[...] = jnp.where(lane < ROPE_DIM, rr, 0.0).astype(BF16)


def _att_post(z, g_kv, cs, tm=512):
    t = z.shape[0]
    tm = _pick(t, tm)
    return pl.pallas_call(
        _att_post_kernel,
        grid=(t // tm,),
        in_specs=[pl.BlockSpec((tm, KV_RANK), lambda i: (i, Q_RANK // KV_RANK)),
                  pl.BlockSpec((tm, LANES), lambda i: (i, (Q_RANK + KV_RANK) // LANES)),
                  pl.BlockSpec((1, KV_RANK), lambda i: (0, 0)),
                  pl.BlockSpec((tm, LANES), lambda i: (i, 0))],
        out_specs=[pl.BlockSpec((tm, KV_RANK), lambda i: (i, 0)),
                   pl.BlockSpec((tm, LANES), lambda i: (i, 0)),
                   pl.BlockSpec((tm, LANES), lambda i: (i, 0))],
        out_shape=[jax.ShapeDtypeStruct((t, KV_RANK), F32),
                   jax.ShapeDtypeStruct((t, LANES), F32),
                   jax.ShapeDtypeStruct((t, LANES), BF16)],
        compiler_params=_params(("parallel",)),
        name="att_post",
    )(z, z, g_kv.reshape(1, KV_RANK), cs)


def _flash_kernel(*refs, mode, scale, tq, tk, nk):
    if mode == "mla":
        qn_ref, qr_ref, cs_ref, kn_ref, kr_ref, v_ref, o_ref, q1_s, q2_s, m_s, l_s, acc_s = refs
    else:
        qn_ref, kn_ref, v_ref, b_ref, o_ref, q1_s, m_s, l_s, acc_s = refs
    i = pl.program_id(1)
    j = pl.program_id(2)

    @pl.when(j == 0)
    def _():
        m_s[...] = jnp.full(m_s.shape, NEG_INF, F32)
        l_s[...] = jnp.zeros(l_s.shape, F32)
        acc_s[...] = jnp.zeros(acc_s.shape, F32)
        q1_s[...] = (qn_ref[...] * scale).astype(BF16)
        if mode == "mla":
            q2_s[...] = (_rope_block(qr_ref[...], cs_ref[...]) * scale).astype(BF16)

    def step(masked):
        s = _dot_nt(q1_s[...], kn_ref[...].astype(BF16))
        if mode == "mla":
            s = s + _dot_nt(q2_s[...], kr_ref[...])
            if masked:
                row = lax.broadcasted_iota(I32, (tq, tk), 0) // CHUNK
                col = lax.broadcasted_iota(I32, (tq, tk), 1) // CHUNK
                s = jnp.where(col <= row, s, NEG_INF)
        else:
            s = s + b_ref[0]
        m_old = m_s[...]
        m_new = jnp.maximum(m_old, jnp.max(s, axis=-1, keepdims=True))
        alpha = jnp.exp(m_old - m_new)
        p = jnp.exp(s - m_new)
        l_s[...] = alpha * l_s[...] + jnp.sum(p, axis=-1, keepdims=True)
        acc_s[...] = alpha * acc_s[...] + _dot(p.astype(BF16), v_ref[...].astype(BF16))
        m_s[...] = m_new

    if mode == "mla":
        @pl.when(j < i)
        def _():
            step(False)

        @pl.when(j == i)
        def _():
            step(True)
    else:
        @pl.when(i + j >= nk - 1)
        def _():
            step(False)

    @pl.when(j == nk - 1)
    def _():
        o_ref[...] = (acc_s[...] / l_s[...]).astype(o_ref.dtype)


def _mla_prompt(q, cs, kv, krp, seq, tq=512):
    nq = seq // tq
    kern = functools.partial(_flash_kernel, mode="mla", scale=(NOPE_DIM + ROPE_DIM) ** -0.5,
                             tq=tq, tk=tq, nk=nq)
    return pl.pallas_call(
        kern,
        grid=(MLA_HEADS, nq, nq),
        in_specs=[pl.BlockSpec((tq, LANES), lambda h, i, j: (i, h)),
                  pl.BlockSpec((tq, LANES), lambda h, i, j: (i, MLA_HEADS + h)),
                  pl.BlockSpec((tq, LANES), lambda h, i, j: (i, 0)),
                  pl.BlockSpec((tq, LANES), lambda h, i, j: (jnp.minimum(i, j), 2 * h)),
                  pl.BlockSpec((tq, LANES), lambda h, i, j: (jnp.minimum(i, j), 0)),
                  pl.BlockSpec((tq, LANES), lambda h, i, j: (jnp.minimum(i, j), 2 * h + 1))],
        out_specs=pl.BlockSpec((tq, LANES), lambda h, i, j: (i, h)),
        out_shape=jax.ShapeDtypeStruct((seq, MLA_HEADS * MLA_VDIM), BF16),
        scratch_shapes=[pltpu.VMEM((tq, LANES), BF16), pltpu.VMEM((tq, LANES), BF16),
                        pltpu.VMEM((tq, 1), F32), pltpu.VMEM((tq, 1), F32), pltpu.VMEM((tq, LANES), F32)],
        compiler_params=_params(("parallel", "parallel", "arbitrary")),
        name="mla_prompt",
    )(q, q, cs, kv, krp, kv)


def _band_prompt(z, bias, seq):
    tq = BAND_TQ
    nq = seq // tq
    nk = 3
    qb0 = 2048 // LANES
    kb0 = 4096 // LANES
    vb0 = 6144 // LANES
    kern = functools.partial(_flash_kernel, mode="band", scale=BAND_HDIM ** -0.5, tq=tq, tk=tq, nk=nk)

    def kidx(i, j):
        return jnp.maximum(i + j - (nk - 1), 0)

    return pl.pallas_call(
        kern,
        grid=(BAND_HEADS, nq, nk),
        in_specs=[pl.BlockSpec((tq, LANES), lambda h, i, j: (i, qb0 + h)),
                  pl.BlockSpec((tq, LANES), lambda h, i, j: (kidx(i, j), kb0 + h)),
                  pl.BlockSpec((tq, LANES), lambda h, i, j: (kidx(i, j), vb0 + h)),
                  pl.BlockSpec((1, tq, tq), lambda h, i, j: (h, 0, j))],
        out_specs=pl.BlockSpec((tq, LANES), lambda h, i, j: (i, h)),
        out_shape=jax.ShapeDtypeStruct((seq, BAND_HEADS * BAND_HDIM), BF16),
        scratch_shapes=[pltpu.VMEM((tq, LANES), BF16),
                        pltpu.VMEM((tq, 1), F32), pltpu.VMEM((tq, 1), F32), pltpu.VMEM((tq, LANES), F32)],
        compiler_params=_params(("parallel", "parallel", "arbitrary")),
        name="band_prompt",
    )(z, z, z, bias)


def _bias_kernel(row_ref, bp_ref, bs_ref, *, tq, wk, sq, sk):
    w = row_ref.shape[-1]
    t = jnp.broadcast_to(row_ref[0], (tq, w))
    t = pltpu.roll(t, 0, 1, stride=1, stride_axis=0)
    bs_ref[0] = t[:sq, :sk]
    tp = t[:, :wk]
    rc = lax.broadcasted_iota(I32, (tq, wk), 0) // CHUNK
    cc = lax.broadcasted_iota(I32, (tq, wk), 1) // CHUNK
    ok = (cc >= rc) & (cc <= rc + BAND_PAST_CHUNKS)
    bp_ref[0] = jnp.where(ok, tp, NEG_INF)


def _band_bias(table, sq, sk):
    tq = BAND_TQ
    wk = 3 * tq
    w = 4 * tq
    h = table.shape[0]
    back = 2 * tq
    n_hi = back - REL_MAX + 1
    mid = table[:, 1:N_REL - 1][:, ::-1]
    n_lo = wk + 1 - n_hi - mid.shape[1]
    row = jnp.concatenate([jnp.broadcast_to(table[:, -1:], (h, n_hi)), mid,
                           jnp.broadcast_to(table[:, :1], (h, n_lo)),
                           jnp.broadcast_to(table[:, -1:], (h, w - wk - 1))], axis=1)
    kern = functools.partial(_bias_kernel, tq=tq, wk=wk, sq=sq, sk=sk)
    return pl.pallas_call(
        kern,
        grid=(h,),
        in_specs=[pl.BlockSpec((1, 1, w), lambda i: (i, 0, 0))],
        out_specs=[pl.BlockSpec((1, tq, wk), lambda i: (i, 0, 0)),
                   pl.BlockSpec((1, sq, sk), lambda i: (i, 0, 0))],
        out_shape=[jax.ShapeDtypeStruct((h, tq, wk), F32), jax.ShapeDtypeStruct((h, sq, sk), F32)],
        compiler_params=_params(("parallel",)),
        name="band_bias",
    )(row.reshape(h, 1, w))


def _attn1_kernel(*refs, has_rope, has_bias, scale):
    it = iter(refs)
    q_ref = next(it)
    if has_rope:
        qr_ref = next(it)
        cs_ref = next(it)
    k_ref = next(it)
    if has_rope:
        kr_ref = next(it)
    v_ref = next(it)
    if has_bias:
        b_ref = next(it)
    o_ref = next(it)
    q = (q_ref[0].astype(F32) * scale).astype(BF16)
    s = _dot_nt(q, k_ref[0].astype(BF16))
    if has_rope:
        q2 = (_rope_block(qr_ref[0], cs_ref[0]) * scale).astype(BF16)
        s = s + _dot_nt(q2, kr_ref[0].astype(BF16))
    if has_bias:
        s = s + b_ref[0]
    m = jnp.max(s, axis=-1, keepdims=True)
    p = jnp.exp(s - m)
    l = jnp.sum(p, axis=-1, keepdims=True)
    o = _dot(p.astype(BF16), v_ref[0].astype(BF16)) / l
    o_ref[0] = o.astype(o_ref.dtype)


def _attn1(q, k, v, *, heads, tq, q0, k0, v0, scale, qr=None, qr0=0, cs=None, kr=None, bias=None, name):
    g, lq, _ = q.shape
    lk = k.shape[1]
    nq = lq // tq
    kq0, kstep = k0
    vq0, vstep = v0
    in_specs = [pl.BlockSpec((1, tq, LANES), lambda b, h, i: (b, i, q0 + h))]
    args = [q]
    if qr is not None:
        in_specs += [pl.BlockSpec((1, tq, LANES), lambda b, h, i: (b, i, qr0 + h)),
                     pl.BlockSpec((1, tq, LANES), lambda b, h, i: (b, i, 0))]
        args += [qr, cs]
    in_specs.append(pl.BlockSpec((1, lk, LANES), lambda b, h, i: (b, 0, kq0 + kstep * h)))
    args.append(k)
    if kr is not None:
        in_specs.append(pl.BlockSpec((1, lk, LANES), lambda b, h, i: (b, 0, 0)))
        args.append(kr)
    in_specs.append(pl.BlockSpec((1, lk, LANES), lambda b, h, i: (b, 0, vq0 + vstep * h)))
    args.append(v)
    if bias is not None:
        in_specs.append(pl.BlockSpec((1, tq, lk), lambda b, h, i: (h, 0, 0)))
        args.append(bias)
    kern = functools.partial(_attn1_kernel, has_rope=qr is not None, has_bias=bias is not None, scale=scale)
    return pl.pallas_call(
        kern,
        grid=(g, heads, nq),
        in_specs=in_specs,
        out_specs=pl.BlockSpec((1, tq, LANES), lambda b, h, i: (b, i, h)),
        out_shape=jax.ShapeDtypeStruct((g, lq, heads * LANES), BF16),
        compiler_params=_params(("parallel", "parallel", "parallel")),
        name=name,
    )(*args)


def _gelu_tanh(y):
    return 0.5 * y * (1.0 + jnp.tanh(0.7978845608028654 * (y + 0.044715 * y * y * y)))


def _s5_kernel(u_ref, wb_ref, wc_ref, cst_ref, d_ref, h0r_ref, h0i_ref, y_ref, hrf_ref, hif_ref,
               bu_s, cr_s, ci_s, *, tm, sw):
    i = pl.program_id(2)

    @pl.when(i == 0)
    def _():
        cr_s[...] = h0r_ref[0, 0]
        ci_s[...] = h0i_ref[0, 0]

    u = u_ref[...]
    bu_s[...] = _dot(u.astype(BF16), wb_ref[0])

    def cst(kk):
        return cst_ref[0, kk * SUBLANES:(kk + 1) * SUBLANES, :]

    def body(r, carry):
        cr, ci = carry
        off = pl.multiple_of(r * SUBLANES, SUBLANES)
        xr = bu_s[pl.ds(off, SUBLANES), 0:sw]
        xi = bu_s[pl.ds(off, SUBLANES), sw:2 * sw]
        for n, d in enumerate((1, 2, 4)):
            ar, ai = cst(2 * n), cst(2 * n + 1)
            sr = pltpu.roll(xr, d, 0)
            si = pltpu.roll(xi, d, 0)
            xr, xi = xr + ar * sr - ai * si, xi + ar * si + ai * sr
        pr, pi_ = cst(6), cst(7)
        xr, xi = xr + pr * cr - pi_ * ci, xi + pr * ci + pi_ * cr
        bu_s[pl.ds(off, SUBLANES), 0:sw] = xr
        bu_s[pl.ds(off, SUBLANES), sw:2 * sw] = xi
        cr = jnp.broadcast_to(xr[SUBLANES - 1:SUBLANES, :], (SUBLANES, sw))
        ci = jnp.broadcast_to(xi[SUBLANES - 1:SUBLANES, :], (SUBLANES, sw))
        return cr, ci

    cr, ci = lax.fori_loop(0, tm // SUBLANES, body, (cr_s[...], ci_s[...]))
    cr_s[...] = cr
    ci_s[...] = ci
    hrf_ref[0, 0] = cr
    hif_ref[0, 0] = ci
    y = _dot(bu_s[...].astype(BF16), wc_ref[0]) + d_ref[...] * u
    y_ref[...] = _gelu_tanh(y)


def _s5_prep(lam_re, lam_im, log_dt, b_re, b_im, c_re, c_im):
    g, p = lam_re.shape
    gs = S5_SLAB_GROUPS
    ns = g // gs
    dt = jnp.exp(log_dt.astype(F32))[:, None]
    lr, li = lam_re.astype(F32), lam_im.astype(F32)
    mag = jnp.exp(lr * dt)
    ab_re, ab_im = mag * jnp.cos(li * dt), mag * jnp.sin(li * dt)
    nr, ni = ab_re - 1.0, ab_im
    den = lr * lr + li * li
    cf_re = (nr * lr + ni * li) / den
    cf_im = (ni * lr - nr * li) / den
    br, bi = b_re.astype(F32), b_im.astype(F32)
    bb_re = cf_re[..., None] * br - cf_im[..., None] * bi
    bb_im = cf_re[..., None] * bi + cf_im[..., None] * br
    eye = jnp.eye(gs, dtype=F32)

    def bdiag_in(bb):
        t = jnp.einsum('sgpc,gh->sgchp', bb.reshape(ns, gs, p, S5_GROUP), eye)
        return t.reshape(ns, gs * S5_GROUP, gs * p)

    def bdiag_out(cc):
        t = jnp.einsum('sgcp,gh->sgphc', cc.reshape(ns, gs, S5_GROUP, p), eye)
        return t.reshape(ns, gs * p, gs * S5_GROUP)

    wb = jnp.concatenate([bdiag_in(bb_re), bdiag_in(bb_im)], axis=-1).astype(BF16)
    wc = jnp.concatenate([bdiag_out(c_re.astype(F32)), bdiag_out(-c_im.astype(F32))], axis=1).astype(BF16)
    ar = ab_re.reshape(ns, gs * p)
    ai = ab_im.reshape(ns, gs * p)
    pw = [(ar, ai)]
    for _ in range(SUBLANES - 1):
        qr_, qi_ = pw[-1]
        pw.append((qr_ * ar - qi_ * ai, qr_ * ai + qi_ * ar))
    rows = jnp.arange(SUBLANES)[None, :, None]

    def masked(v, d):
        return jnp.where(rows >= d, v[:, None, :], 0.0)

    cst = []
    for d in (1, 2, 4):
        cst += [masked(pw[d - 1][0], d), masked(pw[d - 1][1], d)]
    cst.append(jnp.stack([pw[t][0] for t in range(SUBLANES)], axis=1))
    cst.append(jnp.stack([pw[t][1] for t in range(SUBLANES)], axis=1))
    cst = jnp.concatenate(cst, axis=1)
    return wb, wc, cst


def _s5(z, prep, d_skip, h0r, h0i, *, batch, length, row0, tm):
    wb, wc, cst = prep
    ns = wb.shape[0]
    sw = wb.shape[2] // 2
    nt = length // tm
    rb = row0 // tm
    kern = functools.partial(_s5_kernel, tm=tm, sw=sw)
    st_spec = pl.BlockSpec((1, 1, SUBLANES, sw), lambda b, s, i: (b, s, 0, 0))
    y, hrf, hif = pl.pallas_call(
        kern,
        grid=(batch, ns, nt),
        in_specs=[pl.BlockSpec((tm, LANES), lambda b, s, i: (rb + b * nt + i, s)),
                  pl.BlockSpec((1, LANES, 2 * sw), lambda b, s, i: (s, 0, 0)),
                  pl.BlockSpec((1, 2 * sw, LANES), lambda b, s, i: (s, 0, 0)),
                  pl.BlockSpec((1, 8 * SUBLANES, sw), lambda b, s, i: (s, 0, 0)),
                  pl.BlockSpec((1, LANES), lambda b, s, i: (0, s)),
                  st_spec, st_spec],
        out_specs=[pl.BlockSpec((tm, LANES), lambda b, s, i: (b * nt + i, s)), st_spec, st_spec],
        out_shape=[jax.ShapeDtypeStruct((batch * length, ns * LANES), F32),
                   jax.ShapeDtypeStruct((batch, ns, SUBLANES, sw), F32),
                   jax.ShapeDtypeStruct((batch, ns, SUBLANES, sw), F32)],
        scratch_shapes=[pltpu.VMEM((tm, 2 * sw), F32), pltpu.VMEM((SUBLANES, sw), F32),
                        pltpu.VMEM((SUBLANES, sw), F32)],
        compiler_params=_params(("parallel", "parallel", "arbitrary")),
        name="s5_scan",
    )(z, wb, wc, cst, d_skip.reshape(1, -1).astype(F32), h0r, h0i)
    return y, hrf[:, :, 0, :], hif[:, :, 0, :]


def _s5_state_in(h0, ns):
    b = h0.shape[0]
    s = h0.astype(F32).reshape(b, ns, 1, -1)
    return jnp.broadcast_to(s, (b, ns, SUBLANES, s.shape[-1]))


def _hgrn_levels(c):
    ii = np.arange(c)[:, None]
    jj = np.arange(c)[None, :]
    x = ii ^ jj
    lev = np.where(ii > jj, np.floor(np.log2(np.maximum(x, 1))).astype(np.int32), -1)
    lev = np.where(ii == jj, 100, lev)
    return jnp.asarray(lev, dtype=I32)


def _hgrn_kernel(qz_ref, fz_ref, vz_ref, gz_ref, lb_ref, gn_ref, s0_ref, lev_ref, o_ref, sf_ref,
                 st_s, g_s, *, c):
    step = pl.program_id(2)

    @pl.when(step == 0)
    def _():
        st_s[...] = s0_ref[0, 0]

    lb = lb_ref[...]
    fz = fz_ref[...]
    qz = qz_ref[...]
    gz = gz_ref[...]
    v = vz_ref[...]
    logf = jnp.log(lb + (1.0 - lb) * _sigmoid(fz))
    k = (1.0 - lb) * _sigmoid(-fz)
    q = qz * _sigmoid(qz)

    ri = lax.broadcasted_iota(I32, (c, c), 0)
    ci = lax.broadcasted_iota(I32, (c, c), 1)
    tri = jnp.where(ci <= ri, 1.0, 0.0).astype(BF16)
    hi = logf.astype(BF16)
    r1 = logf - hi.astype(F32)
    mid = r1.astype(BF16)
    lo = (r1 - mid.astype(F32)).astype(BF16)
    gcum = _dot(tri, hi) + _dot(tri, mid) + _dot(tri, lo)
    g_s[...] = gcum

    sub = lax.broadcasted_iota(I32, (c, LANES), 0)
    lev = lev_ref[...]
    qb = q.astype(BF16)
    kb = k.astype(BF16)
    a = jnp.where(lev == 100, _dot_nt(qb, kb), 0.0)
    nlev = int(math.log2(c))
    for l in range(nlev):
        m = 1 << l
        if 2 * m <= SUBLANES:
            pos = sub & (2 * m - 1)
            ref = gcum
            for off in range(-(m - 1), m + 1):
                if off == 0:
                    continue
                cand = pltpu.roll(gcum, off % c, 0)
                ref = jnp.where(pos == (m - 1 + off), cand, ref)
        else:
            blocks = []
            for b0 in range(0, c, 2 * m):
                r = b0 + m - 1
                blocks.append(jnp.broadcast_to(g_s[r:r + 1, :], (2 * m, LANES)))
            ref = blocks[0] if len(blocks) == 1 else jnp.concatenate(blocks, axis=0)
        f = jnp.exp(-jnp.abs(gcum - ref))
        pl_ = _dot_nt((q * f).astype(BF16), (k * f).astype(BF16))
        a = jnp.where(lev == l, pl_, a)

    st = st_s[...]
    o = _dot_nt((q * jnp.exp(gcum)).astype(BF16), st.astype(BF16)) + _dot(a.astype(BF16), v.astype(BF16))
    glast = g_s[c - 1:c, :]
    kdec = k * jnp.exp(glast - gcum)
    st_new = st * jnp.exp(glast) + _dot(v.T.astype(BF16), kdec.astype(BF16))
    st_s[...] = st_new
    sf_ref[0, 0] = st_new
    ms = jnp.mean(o * o, axis=-1, keepdims=True)
    on = o * lax.rsqrt(ms + EPS) * gn_ref[...]
    o_ref[...] = (on * (gz * _sigmoid(gz))).astype(o_ref.dtype)


def _hgrn(z, lb, g_hg, s0t, *, batch, length, row0, c):
    nc = length // c
    rb = row0 // c
    hw = HG_HEADS
    kern = functools.partial(_hgrn_kernel, c=c)

    def zspec(col0):
        return pl.BlockSpec((c, LANES), lambda b, h, i: (rb + b * nc + i, col0 + h))

    st_spec = pl.BlockSpec((1, 1, HG_VDIM, HG_KDIM), lambda b, h, i: (b, h, 0, 0))
    return pl.pallas_call(
        kern,
        grid=(batch, hw, nc),
        in_specs=[zspec(hw), zspec(2 * hw), zspec(3 * hw), zspec(4 * hw),
                  pl.BlockSpec((1, LANES), lambda b, h, i: (0, h)),
                  pl.BlockSpec((1, LANES), lambda b, h, i: (0, 0)),
                  st_spec,
                  pl.BlockSpec((c, c), lambda b, h, i: (0, 0))],
        out_specs=[pl.BlockSpec((c, LANES), lambda b, h, i: (b * nc + i, h)), st_spec],
        out_shape=[jax.ShapeDtypeStruct((batch * length, hw * HG_VDIM), BF16),
                   jax.ShapeDtypeStruct((batch, hw, HG_VDIM, HG_KDIM), F32)],
        scratch_shapes=[pltpu.VMEM((HG_VDIM, HG_KDIM), F32), pltpu.VMEM((c, LANES), F32)],
        compiler_params=_params(("parallel", "parallel", "arbitrary")),
        name="hgrn2",
    )(z, z, z, z, lb.reshape(1, -1), g_hg.reshape(1, -1), s0t, _hgrn_levels(c))


def _router_kernel(x_ref, g_ref, wh_ref, wl_ref, b_ref, xp_ref, idx_ref, wt_ref, cnt_ref, cnt_s,
                   *, tm, n_exp):
    @pl.when(pl.program_id(0) == 0)
    def _():
        cnt_s[...] = jnp.zeros(cnt_s.shape, F32)

    x = x_ref[...]
    ms = jnp.mean(x * x, axis=-1, keepdims=True)
    xn = x * lax.rsqrt(ms + EPS) * g_ref[...]
    xh = xn.astype(BF16)
    xhf = xh.astype(F32)
    bits = lax.bitcast_convert_type(xhf, U32)
    half = x.shape[1] // 2
    xp_ref[...] = (bits[:, :half] >> 16) | (bits[:, half:] & jnp.uint32(0xFFFF0000))
    xl = (xn - xhf).astype(BF16)
    wh = wh_ref[...]
    logits = _dot(xh, wh) + _dot(xl, wh) + _dot(xh, wl_ref[...]) + b_ref[...]
    lane = lax.broadcasted_iota(I32, (tm, LANES), 1)
    lanef = lane.astype(F32)
    neg = jnp.float32(-jnp.inf)
    l = jnp.where(lane < n_exp, logits, neg)
    vals, sels, idxs = [], [], []
    for _ in range(TOP_K):
        mx = jnp.max(l, axis=-1, keepdims=True)
        ix = jnp.min(jnp.where(l == mx, lanef, float(LANES)), axis=-1, keepdims=True)
        sk = lanef == ix
        l = jnp.where(sk, neg, l)
        vals.append(mx)
        sels.append(sk)
        idxs.append(ix.astype(I32))
    es = [jnp.exp(v - vals[0]) for v in vals]
    den = es[0]
    for e in es[1:]:
        den = den + e
    sel = jnp.zeros((tm, LANES), F32)
    for sk in sels:
        sel = jnp.where(sk, 1.0, sel)
    ri = lax.broadcasted_iota(I32, (tm, tm), 0)
    ci = lax.broadcasted_iota(I32, (tm, tm), 1)
    tri = jnp.where(ci < ri, 1.0, 0.0).astype(BF16)
    rank = _dot(tri, sel.astype(BF16)) + cnt_s[...]
    cnt = cnt_s[...] + jnp.sum(sel, axis=0, keepdims=True)
    cnt_s[...] = cnt
    cnt_ref[...] = jnp.broadcast_to(cnt, cnt_ref.shape).astype(I32)
    io = jnp.zeros((tm, LANES), I32)
    wo = jnp.zeros((tm, LANES), F32)
    for kk in range(TOP_K):
        rk = jnp.sum(jnp.where(sels[kk], rank, 0.0), axis=-1, keepdims=True).astype(I32)
        io = jnp.where(lane == kk, idxs[kk], io)
        io = jnp.where(lane == TOP_K + kk, rk, io)
        wo = jnp.where(lane == kk, es[kk] / den, wo)
    idx_ref[...] = io
    wt_ref[...] = wo


def _router(x, g, w_router, b_router, tm=256):
    t, d = x.shape
    n_exp = w_router.shape[1]
    tm = _pick(t, tm)
    wpad = jnp.zeros((d, LANES), F32).at[:, :n_exp].set(w_router.astype(F32))
    wh = wpad.astype(BF16)
    wl = (wpad - wh.astype(F32)).astype(BF16)
    bpad = jnp.zeros((1, LANES), F32).at[0, :n_exp].set(b_router.astype(F32))
    kern = functools.partial(_router_kernel, tm=tm, n_exp=n_exp)
    return pl.pallas_call(
        kern,
        grid=(t // tm,),
        in_specs=[pl.BlockSpec((tm, d), lambda i: (i, 0)),
                  pl.BlockSpec((1, d), lambda i: (0, 0)),
                  pl.BlockSpec((d, LANES), lambda i: (0, 0)),
                  pl.BlockSpec((d, LANES), lambda i: (0, 0)),
                  pl.BlockSpec((1, LANES), lambda i: (0, 0))],
        out_specs=[pl.BlockSpec((tm, d // 2), lambda i: (i, 0)),
                   pl.BlockSpec((tm, LANES), lambda i: (i, 0)),
                   pl.BlockSpec((tm, LANES), lambda i: (i, 0)),
                   pl.BlockSpec((SUBLANES, LANES), lambda i: (0, 0))],
        out_shape=[jax.ShapeDtypeStruct((t, d // 2), U32),
                   jax.ShapeDtypeStruct((t, LANES), I32),
                   jax.ShapeDtypeStruct((t, LANES), F32),
                   jax.ShapeDtypeStruct((SUBLANES, LANES), I32)],
        scratch_shapes=[pltpu.VMEM((1, LANES), F32)],
        compiler_params=_params(("arbitrary",)),
        name="moe_router",
    )(x, g.reshape(1, d), wh, wl, bpad)


def _dispatch_kernel(slot_ref, x_ref, zero_ref, xs_ref, sem, *, tm):
    del zero_ref

    def row_copy(r, kk):
        return pltpu.make_async_copy(x_ref.at[pl.ds(r, 1)],
                                     xs_ref.at[pl.ds(slot_ref[r * TOP_K + kk], 1)], sem)

    def start(r, c):
        for kk in range(TOP_K):
            row_copy(r, kk).start()
        return c

    def wait(r, c):
        for kk in range(TOP_K):
            row_copy(r, kk).wait()
        return c

    lax.fori_loop(0, tm, start, 0)
    lax.fori_loop(0, tm, wait, 0)


def _dispatch(xp, slots, n_rows, tm=256):
    t, dh = xp.shape
    tm = _pick(t, tm)
    kern = functools.partial(_dispatch_kernel, tm=tm)
    return pl.pallas_call(
        kern,
        grid=(t // tm,),
        in_specs=[pl.BlockSpec((tm * TOP_K,), lambda i: (i,), memory_space=pltpu.SMEM),
                  pl.BlockSpec((tm, dh), lambda i: (i, 0)),
                  pl.BlockSpec(memory_space=pl.ANY)],
        out_specs=pl.BlockSpec(memory_space=pl.ANY),
        out_shape=jax.ShapeDtypeStruct((n_rows, dh), U32),
        scratch_shapes=[pltpu.SemaphoreType.DMA],
        input_output_aliases={2: 0},
        compiler_params=pltpu.CompilerParams(dimension_semantics=("arbitrary",), vmem_limit_bytes=VMEM_LIMIT,
                                             has_side_effects=True),
        name="moe_dispatch",
    )(slots.reshape(-1), xp, jnp.zeros((n_rows, dh), U32))


def _unpack(u):
    lo = lax.bitcast_convert_type(u << 16, F32).astype(BF16)
    hi = lax.bitcast_convert_type(u & jnp.uint32(0xFFFF0000), F32).astype(BF16)
    return lo, hi


def _ffn1_kernel(te_ref, nu_ref, xs_ref, wg_ref, wl_ref, bg_ref, bl_ref, a_ref, wg_s, wl_s):
    t = pl.program_id(1)
    prev = te_ref[jnp.maximum(t - 1, 0)]
    changed = (t == 0) | (te_ref[t] != prev)

    @pl.when(changed)
    def _():
        wg_s[...] = wg_ref[...].astype(BF16)
        wl_s[...] = wl_ref[...].astype(BF16)

    @pl.when(t < nu_ref[0])
    def _():
        lo, hi = _unpack(xs_ref[...])
        half = lo.shape[1]
        hg = _dot(lo, wg_s[:half, :]) + _dot(hi, wg_s[half:, :]) + bg_ref[...]
        hl = _dot(lo, wl_s[:half, :]) + _dot(hi, wl_s[half:, :]) + bl_ref[...]
        xg = jnp.minimum(hg, SWIGLU_LIMIT)
        xl = jnp.clip(hl, -SWIGLU_LIMIT, SWIGLU_LIMIT)
        a_ref[...] = (xg * _sigmoid(SWIGLU_ALPHA * xg) * (xl + 1.0)).astype(a_ref.dtype)

    @pl.when(t >= nu_ref[0])
    def _():
        a_ref[...] = jnp.zeros(a_ref.shape, a_ref.dtype)


def _ffn1(xs, tile_e, n_used, w_e1, b_e1, layer, tm, nc=256):
    n_rows, dh = xs.shape
    d = 2 * dh
    f = w_e1.shape[-1] // 2
    nc = _pick(f, nc)
    nj = f // nc
    n_tiles = n_rows // tm
    b1 = b_e1.reshape(b_e1.shape[0], b_e1.shape[1], 1, 2 * f)

    def tcl(t, nu):
        return jnp.minimum(t, nu[0] - 1)

    return pl.pallas_call(
        _ffn1_kernel,
        grid_spec=pltpu.PrefetchScalarGridSpec(
            num_scalar_prefetch=2,
            grid=(nj, n_tiles),
            in_specs=[pl.BlockSpec((tm, dh), lambda j, t, te, nu: (tcl(t, nu), 0)),
                      pl.BlockSpec((None, None, d, nc), lambda j, t, te, nu: (layer, te[t], 0, j)),
                      pl.BlockSpec((None, None, d, nc), lambda j, t, te, nu: (layer, te[t], 0, nj + j)),
                      pl.BlockSpec((None, None, 1, nc), lambda j, t, te, nu: (layer, te[t], 0, j)),
                      pl.BlockSpec((None, None, 1, nc), lambda j, t, te, nu: (layer, te[t], 0, nj + j))],
            out_specs=pl.BlockSpec((tm, nc), lambda j, t, te, nu: (t, j)),
            scratch_shapes=[pltpu.VMEM((d, nc), BF16), pltpu.VMEM((d, nc), BF16)]),
        out_shape=jax.ShapeDtypeStruct((n_rows, f), BF16),
        compiler_params=_params(("arbitrary", "arbitrary")),
        name="moe_ffn1",
    )(tile_e, n_used, xs, w_e1, w_e1, b1, b1)


def _ffn2_kernel(te_ref, nu_ref, a_ref, w_ref, b_ref, o_ref, w_s):
    t = pl.program_id(1)
    prev = te_ref[jnp.maximum(t - 1, 0)]
    changed = (t == 0) | (te_ref[t] != prev)

    @pl.when(changed)
    def _():
        w_s[...] = w_ref[...].astype(BF16)

    @pl.when(t < nu_ref[0])
    def _():
        o_ref[...] = _dot(a_ref[...], w_s[...]) + b_ref[...]

    @pl.when(t >= nu_ref[0])
    def _():
        o_ref[...] = jnp.zeros(o_ref.shape, o_ref.dtype)


def _ffn2(a, tile_e, n_used, w_e2, b_e2, layer, tm, nc=1024):
    n_rows, f = a.shape
    d = w_e2.shape[-1]
    nc = _pick(d, nc)
    nj = d // nc
    n_tiles = n_rows // tm
    b2 = b_e2.reshape(b_e2.shape[0], b_e2.shape[1], 1, d)

    def tcl(t, nu):
        return jnp.minimum(t, nu[0] - 1)

    return pl.pallas_call(
        _ffn2_kernel,
        grid_spec=pltpu.PrefetchScalarGridSpec(
            num_scalar_prefetch=2,
            grid=(nj, n_tiles),
            in_specs=[pl.BlockSpec((tm, f), lambda j, t, te, nu: (tcl(t, nu), 0)),
                      pl.BlockSpec((None, None, f, nc), lambda j, t, te, nu: (layer, te[t], 0, j)),
                      pl.BlockSpec((None, None, 1, nc), lambda j, t, te, nu: (layer, te[t], 0, j))],
            out_specs=pl.BlockSpec((tm, nc), lambda j, t, te, nu: (t, j)),
            scratch_shapes=[pltpu.VMEM((f, nc), BF16)]),
        out_shape=jax.ShapeDtypeStruct((n_rows, d), F32),
        compiler_params=_params(("arbitrary", "arbitrary")),
        name="moe_ffn2",
    )(tile_e, n_used, a, w_e2, b2)


def _combine_kernel(slot_ref, x_ref, w_ref, o_hbm, y_ref, buf, sem, *, tm):
    def row_copy(r, kk):
        return pltpu.make_async_copy(o_hbm.at[pl.ds(slot_ref[r * TOP_K + kk], 1)],
                                     buf.at[kk, pl.ds(r, 1)], sem)

    def start(r, c):
        for kk in range(TOP_K):
            row_copy(r, kk).start()
        return c

    def wait(r, c):
        for kk in range(TOP_K):
            row_copy(r, kk).wait()
        return c

    lax.fori_loop(0, tm, start, 0)
    lax.fori_loop(0, tm, wait, 0)
    w = w_ref[...]
    acc = w[:, 0:1] * buf[0]
    for kk in range(1, TOP_K):
        acc = acc + w[:, kk:kk + 1] * buf[kk]
    y_ref[...] = x_ref[...] + acc


def _combine(x, wts, slots, o, tm=128):
    t, d = x.shape
    tm = _pick(t, tm)
    kern = functools.partial(_combine_kernel, tm=tm)
    return pl.pallas_call(
        kern,
        grid=(t // tm,),
        in_specs=[pl.BlockSpec((tm * TOP_K,), lambda i: (i,), memory_space=pltpu.SMEM),
                  pl.BlockSpec((tm, d), lambda i: (i, 0)),
                  pl.BlockSpec((tm, LANES), lambda i: (i, 0)),
                  pl.BlockSpec(memory_space=pl.ANY)],
        out_specs=pl.BlockSpec((tm, d), lambda i: (i, 0)),
        out_shape=jax.ShapeDtypeStruct((t, d), F32),
        scratch_shapes=[pltpu.VMEM((TOP_K, tm, d), F32), pltpu.SemaphoreType.DMA],
        compiler_params=_params(("arbitrary",)),
        name="moe_combine",
    )(slots.reshape(-1), x, wts, o)


def _moe(x, g, w_router, b_router, w_e1, b_e1, w_e2, b_e2, layer):
    t = x.shape[0]
    n_exp = w_router.shape[1]
    tm = MOE_TM
    xp, idxrank, wts, cnt = _router(x, g, w_router, b_router)
    idx = idxrank[:, :TOP_K]
    rank = idxrank[:, TOP_K:2 * TOP_K]
    cnt = cnt[0, :n_exp]
    ptiles = (cnt + tm - 1) // tm
    tend = jnp.cumsum(ptiles)
    pstart = (tend - ptiles) * tm
    slots = (pstart[idx] + rank).astype(I32)
    n_tiles = (t * TOP_K + n_exp * (tm - 1)) // tm
    n_used = tend[-1:].astype(I32)
    tid = jnp.minimum(jnp.arange(n_tiles, dtype=I32), n_used[0] - 1)
    tile_e = jnp.minimum(jnp.sum(tid[:, None] >= tend[None, :], axis=1), n_exp - 1).astype(I32)
    xs = _dispatch(xp, slots, n_tiles * tm)
    a = _ffn1(xs, tile_e, n_used, w_e1, b_e1, layer, tm)
    o = _ffn2(a, tile_e, n_used, w_e2, b_e2, layer, tm)
    return _combine(x, wts, slots, o)


def _att_weights(w_in, w_uq):
    d = w_in.shape[0]
    bw = BAND_HEADS * BAND_HDIM
    o1, o2, o3 = Q_RANK, Q_RANK + KV_RANK, Q_RANK + KV_RANK + ROPE_DIM
    half = ROPE_DIM // 2
    kr = w_in[:, o2:o3]
    krs = jnp.concatenate([kr[:, half:], kr[:, :half]], axis=1)
    pad = jnp.zeros((d, 2048 - (o3 + ROPE_DIM)), w_in.dtype)
    w_in2 = jnp.concatenate([w_in[:, :o3], krs, pad, w_in[:, o3:o3 + 3 * bw]], axis=1).astype(BF16)
    w3 = w_uq.reshape(Q_RANK, MLA_HEADS, NOPE_DIM + ROPE_DIM)
    r = w3[:, :, NOPE_DIM:]
    rs = jnp.concatenate([r[:, :, half:], r[:, :, :half]], axis=-1)
    w_uq2 = jnp.concatenate([w3[:, :, :NOPE_DIM].reshape(Q_RANK, -1),
                             jnp.concatenate([r, rs], axis=-1).reshape(Q_RANK, -1)], axis=1).astype(BF16)
    return w_in2, w_uq2


def _att_layer(x, seq, nb, ls, past, cs, c_lat, c_kr, c_bk, c_bv, g_mix, w_in, g_q, w_uq, g_kv, w_ukv,
               rel_bias, w_out):
    ts = nb * ls
    keep = c_bk.shape[1]
    assert past % CHUNK == 0 and ls <= CHUNK and keep == BAND_PAST_CHUNKS * CHUNK and past >= keep
    assert 2 * BAND_TQ == BAND_PAST_CHUNKS * CHUNK and seq % 512 == 0
    w_in2, w_uq2 = _att_weights(w_in, w_uq)
    z = _mm(x, w_in2, gain=g_mix, name="att_in")
    lat, krr, krp = _att_post(z, g_kv, cs)
    q = _mm(z, w_uq2, gain=g_q, xcol=0, name="att_uq")
    w_ukv2 = w_ukv.astype(BF16)
    kv_p = _mm(lat, w_ukv2, rows=seq, out_dtype=BF16, name="att_ukv_p")
    o_a = _mla_prompt(q, cs, kv_p, krp, seq)
    bias_p, bias_s = _band_bias(rel_bias, ls, keep + ls)
    o_b = _band_prompt(z, bias_p, seq)
    lat_s = lat[seq:].reshape(nb, ls, KV_RANK)
    lat_all = jnp.concatenate([c_lat.astype(F32), lat_s], axis=1)
    nk = past + ls
    kv_s = _mm(lat_all.reshape(nb * nk, KV_RANK), w_ukv2, out_dtype=BF16, tm=528, name="att_ukv_s")
    kv_s = kv_s.reshape(nb, nk, -1)
    c_krp = jnp.concatenate([c_kr.astype(F32), jnp.zeros(c_kr.shape[:-1] + (LANES - ROPE_DIM,), F32)], axis=-1)
    kr_all = jnp.concatenate([c_krp.astype(BF16), krp[seq:].reshape(nb, ls, LANES)], axis=1)
    q_s = q[seq:].reshape(nb, ls, -1)
    cs_s = cs[seq:].reshape(nb, ls, LANES)
    o_as = _attn1(q_s, kv_s, kv_s, heads=MLA_HEADS, tq=ls, q0=0, k0=(0, 2), v0=(1, 2),
                  scale=(NOPE_DIM + ROPE_DIM) ** -0.5, qr=q_s, qr0=MLA_HEADS, cs=cs_s, kr=kr_all,
                  name="mla_sample")
    z_s = z[seq:].reshape(nb, ls, -1)
    bw = BAND_HEADS * BAND_HDIM
    kb_all = jnp.concatenate([c_bk.reshape(nb, keep, bw).astype(F32), z_s[:, :, 4096:4096 + bw]], axis=1)
    vb_all = jnp.concatenate([c_bv.reshape(nb, keep, bw).astype(F32), z_s[:, :, 6144:6144 + bw]], axis=1)
    o_bs = _attn1(z_s, kb_all, vb_all, heads=BAND_HEADS, tq=ls, q0=2048 // LANES, k0=(0, 1), v0=(0, 1),
                  scale=BAND_HDIM ** -0.5, bias=bias_s, name="band_sample")
    o = jnp.concatenate([jnp.concatenate([o_a, o_b], axis=1),
                         jnp.concatenate([o_as.reshape(ts, -1), o_bs.reshape(ts, -1)], axis=1)], axis=0)
    x = _mm(o, w_out.astype(BF16), residual=x, name="att_out")
    outs = dict(
        lat_p=lat[:seq], kr_p=krr[:seq, :ROPE_DIM],
        bk_p=z[seq - min(keep, seq):seq, 4096:4096 + bw], bv_p=z[seq - min(keep, seq):seq, 6144:6144 + bw],
        lat_s=lat_s, kr_s=krr[seq:, :ROPE_DIM].reshape(nb, ls, ROPE_DIM),
        bk_s=z_s[:, :, 4096:4096 + bw], bv_s=z_s[:, :, 6144:6144 + bw])
    return x, outs


def _rec_layer(x, seq, nb, ls, s5r, s5i, hg0, lb, g_mix, w_in, lam_re, lam_im, log_dt, b_re, b_im, c_re, c_im,
               d_skip, w_glu, g_hg, w_out):
    z = _mm(x, w_in.astype(BF16), gain=g_mix, name="rec_in")
    prep = _s5_prep(lam_re, lam_im, log_dt, b_re, b_im, c_re, c_im)
    ns = prep[0].shape[0]
    g, p = lam_re.shape
    zeros_p = jnp.zeros((1, ns, SUBLANES, prep[0].shape[2] // 2), F32)
    y_p, hr_p, hi_p = _s5(z, prep, d_skip, zeros_p, zeros_p, batch=1, length=seq, row0=0, tm=_pick(seq, 256))
    y_s, hr_s, hi_s = _s5(z, prep, d_skip, _s5_state_in(s5r, ns), _s5_state_in(s5i, ns),
                          batch=nb, length=ls, row0=seq, tm=ls)
    y = jnp.concatenate([y_p, y_s], axis=0)
    o_c = _mm(y, w_glu.astype(BF16), glu=y, out_dtype=BF16, name="rec_glu")
    hz = jnp.zeros((1, HG_HEADS, HG_VDIM, HG_KDIM), F32)
    od_p, sf_p = _hgrn(z, lb, g_hg, hz, batch=1, length=seq, row0=0, c=_pick(seq, 256))
    od_s, sf_s = _hgrn(z, lb, g_hg, jnp.swapaxes(hg0.astype(F32), -1, -2), batch=nb, length=ls, row0=seq, c=ls)
    o = jnp.concatenate([o_c, jnp.concatenate([od_p, od_s], axis=0)], axis=1)
    x = _mm(o, w_out.astype(BF16), residual=x, name="rec_out")
    outs = dict(sr_p=hr_p.reshape(1, g, p), si_p=hi_p.reshape(1, g, p), hg_p=jnp.swapaxes(sf_p, -1, -2),
                sr_s=hr_s.reshape(nb, g, p), si_s=hi_s.reshape(nb, g, p), hg_s=jnp.swapaxes(sf_s, -1, -2))
    return x, outs


def _mem_layer(x, seq, nb, ls, mem, cmk, cmv, g_x, g_m, w_mq, w_mkv, w_mo):
    n_mem = mem.shape[0]
    mw = MEM_HEADS * MEM_HDIM
    kvm = _mm(mem, w_mkv.astype(BF16), gain=g_m, tm=n_mem, name="mem_kv")
    q = _mm(x, w_mq.astype(BF16), gain=g_x, out_dtype=BF16, name="mem_q")
    scale = MEM_HDIM ** -0.5
    kvm3 = kvm.reshape(1, n_mem, 2 * mw)
    o_p = _attn1(q[:seq].reshape(1, seq, mw), kvm3, kvm3, heads=MEM_HEADS, tq=_pick(seq, 512), q0=0,
                 k0=(0, 1), v0=(MEM_HEADS, 1), scale=scale, name="mem_attn_p")
    o_s = _attn1(q[seq:].reshape(nb, ls, mw), cmk.reshape(nb, n_mem, mw), cmv.reshape(nb, n_mem, mw),
                 heads=MEM_HEADS, tq=ls, q0=0, k0=(0, 1), v0=(0, 1), scale=scale, name="mem_attn_s")
    o = jnp.concatenate([o_p.reshape(seq, mw), o_s.reshape(nb * ls, mw)], axis=0)
    x = _mm(o, w_mo.astype(BF16), residual=x, name="mem_out")
    mk = kvm[:, :mw].reshape(1, n_mem, MEM_HEADS, MEM_HDIM)
    mv = kvm[:, mw:].reshape(1, n_mem, MEM_HEADS, MEM_HDIM)
    return x, mk, mv


def kernel(x_prompt, x_sample, cache_mla_latent, cache_mla_krope, cache_band_k, cache_band_v, cache_mem_k,
           cache_mem_v, state_s5_re, state_s5_im, state_hgrn, mem_prompt, g_mix, g_xattn, g_mem, g_moe,
           g_final, w_in_att, g_q, w_uq, g_kv, w_ukv, rel_bias, w_out_att, w_in_rec, s5_lam_re, s5_lam_im,
           s5_log_dt, s5_b_re, s5_b_im, s5_c_re, s5_c_im, s5_d, w_glu, hg_lb_logits, g_hg, w_out_rec, w_mq,
           w_mkv, w_mo, w_router, b_router, w_e1, b_e1, w_e2, b_e2):
    bp, seq, d = x_prompt.shape
    nb, ls, _ = x_sample.shape
    assert bp == 1
    past = cache_mla_latent.shape[2]
    depth = g_mix.shape[0]
    x = jnp.concatenate([x_prompt.reshape(seq, d), x_sample.reshape(nb * ls, d)], axis=0).astype(F32)
    pos = jnp.concatenate([jnp.arange(seq), jnp.tile(past + jnp.arange(ls), nb)])
    cs = _rope_table(pos)
    probs = jax.nn.softmax(hg_lb_logits.astype(F32), axis=0)
    lb_all = jnp.cumsum(probs, axis=0) - probs
    att, rec, mks, mvs = [], [], [], []
    for l in range(depth):
        j = l // 2
        if l % 2 == 0:
            x, o = _att_layer(x, seq, nb, ls, past, cs, cache_mla_latent[j], cache_mla_krope[j],
                              cache_band_k[j], cache_band_v[j], g_mix[l], w_in_att[j], g_q[j], w_uq[j],
                              g_kv[j], w_ukv[j], rel_bias[j], w_out_att[j])
            att.append(o)
        else:
            x, o = _rec_layer(x, seq, nb, ls, state_s5_re[j], state_s5_im[j], state_hgrn[j], lb_all[j],
                              g_mix[l], w_in_rec[j], s5_lam_re[j], s5_lam_im[j], s5_log_dt[j], s5_b_re[j],
                              s5_b_im[j], s5_c_re[j], s5_c_im[j], s5_d[j], w_glu[j], g_hg[j], w_out_rec[j])
            rec.append(o)
        x, mk, mv = _mem_layer(x, seq, nb, ls, mem_prompt[0], cache_mem_k[l], cache_mem_v[l], g_xattn[l],
                               g_mem[l], w_mq[l], w_mkv[l], w_mo[l])
        mks.append(mk)
        mvs.append(mv)
        x = _moe(x, g_moe[l], w_router[l], b_router[l], w_e1, b_e1, w_e2, b_e2, l)
    y = _rmsnorm_rows(x, g_final)

    def st(lst, key, shape):
        return jnp.stack([o[key].reshape(shape) for o in lst], axis=0)

    keep_p = min(BAND_PAST_CHUNKS * CHUNK, seq)
    bshape_p = (1, keep_p, BAND_HEADS, BAND_HDIM)
    bshape_s = (nb, ls, BAND_HEADS, BAND_HDIM)
    gsz, psz = s5_lam_re.shape[1:]
    hshape = (HG_HEADS, HG_KDIM, HG_VDIM)
    return (y[:seq].reshape(1, seq, d), y[seq:].reshape(nb, ls, d),
            st(att, "lat_p", (1, seq, KV_RANK)), st(att, "kr_p", (1, seq, ROPE_DIM)),
            st(att, "bk_p", bshape_p), st(att, "bv_p", bshape_p),
            jnp.stack(mks, 0), jnp.stack(mvs, 0),
            st(rec, "sr_p", (1, gsz, psz)), st(rec, "si_p", (1, gsz, psz)), st(rec, "hg_p", (1,) + hshape),
            st(att, "lat_s", (nb, ls, KV_RANK)), st(att, "kr_s", (nb, ls, ROPE_DIM)),
            st(att, "bk_s", bshape_s), st(att, "bv_s", bshape_s),
            st(rec, "sr_s", (nb, gsz, psz)), st(rec, "si_s", (nb, gsz, psz)), st(rec, "hg_s", (nb,) + hshape))
```

```python
import functools
import math

import numpy as np
import jax
import jax.numpy as jnp
from jax import lax
from jax.experimental import pallas as pl
from jax.experimental.pallas import tpu as pltpu

F32 = jnp.float32
BF16 = jnp.bfloat16
I32 = jnp.int32
U32 = jnp.uint32

CHUNK = 64
MLA_HEADS = 16
Q_RANK = 1024
KV_RANK = 512
NOPE_DIM = 128
ROPE_DIM = 64
MLA_VDIM = 128
ROPE_BASE = 10000.0
BAND_HEADS = 16
BAND_HDIM = 128
BAND_PAST_CHUNKS = 8
REL_MAX = 256
N_REL = REL_MAX + CHUNK
S5_GROUP = 16
S5_STATE = 64
HG_HEADS = 16
HG_KDIM = 128
HG_VDIM = 128
MEM_HEADS = 4
MEM_HDIM = 128
TOP_K = 4
SWIGLU_LIMIT = 7.0
SWIGLU_ALPHA = 1.702
EPS = 1e-6
NEG_INF = -1e30

LANES = 128
SUBLANES = 8
VMEM_LIMIT = 56 * 1024 * 1024
S5_SLAB_GROUPS = 8
MOE_TM = 512
BAND_TQ = 256


def _params(sem):
    return pltpu.CompilerParams(dimension_semantics=sem, vmem_limit_bytes=VMEM_LIMIT)


def _dot(a, b):
    return jnp.dot(a, b, preferred_element_type=F32)


def _dot_nt(a, b):
    return lax.dot_general(a, b, (((1,), (1,)), ((), ())), preferred_element_type=F32)


def _sigmoid(x):
    return 1.0 / (1.0 + jnp.exp(-x))


def _pick(n, pref):
    t = min(pref, n)
    while n % t:
        t //= 2
    return t


def _mm_kernel(*refs, has_gain, has_bias, has_res, has_glu):
    it = iter(refs)
    x_ref = next(it)
    w_ref = next(it)
    g_ref = next(it) if has_gain else None
    b_ref = next(it) if has_bias else None
    r_ref = next(it) if has_res else None
    y_ref = next(it) if has_glu else None
    o_ref = next(it)
    xs_ref = next(it)

    @pl.when(pl.program_id(1) == 0)
    def _():
        x = x_ref[...].astype(F32)
        if has_gain:
            ms = jnp.mean(x * x, axis=-1, keepdims=True)
            x = x * lax.rsqrt(ms + EPS) * g_ref[...]
        xs_ref[...] = x.astype(BF16)

    acc = _dot(xs_ref[...], w_ref[...])
    if has_bias:
        acc = acc + b_ref[...]
    if has_glu:
        acc = y_ref[...].astype(F32) * _sigmoid(acc)
    if has_res:
        acc = acc + r_ref[...]
    o_ref[...] = acc.astype(o_ref.dtype)


def _mm(x, w, *, gain=None, bias=None, residual=None, glu=None, xcol=0, rows=None, row0=0,
        tm=512, tn=512, out_dtype=F32, name="mm"):
    k, n = w.shape
    m = x.shape[0] if rows is None else rows
    tm = _pick(m, tm)
    tn = _pick(n, tn)
    assert row0 % tm == 0 and x.shape[1] % k == 0
    rb = row0 // tm
    in_specs = [pl.BlockSpec((tm, k), lambda i, j: (i + rb, xcol)),
                pl.BlockSpec((k, tn), lambda i, j: (0, j))]
    args = [x, w]
    if gain is not None:
        in_specs.append(pl.BlockSpec((1, k), lambda i, j: (0, 0)))
        args.append(gain.reshape(1, k).astype(F32))
    if bias is not None:
        in_specs.append(pl.BlockSpec((1, tn), lambda i, j: (0, j)))
        args.append(bias.reshape(1, n).astype(F32))
    if residual is not None:
        in_specs.append(pl.BlockSpec((tm, tn), lambda i, j: (i, j)))
        args.append(residual)
    if glu is not None:
        in_specs.append(pl.BlockSpec((tm, tn), lambda i, j: (i, j)))
        args.append(glu)
    kern = functools.partial(_mm_kernel, has_gain=gain is not None, has_bias=bias is not None,
                             has_res=residual is not None, has_glu=glu is not None)
    return pl.pallas_call(
        kern,
        grid=(m // tm, n // tn),
        in_specs=in_specs,
        out_specs=pl.BlockSpec((tm, tn), lambda i, j: (i, j)),
        out_shape=jax.ShapeDtypeStruct((m, n), out_dtype),
        scratch_shapes=[pltpu.VMEM((tm, k), BF16)],
        compiler_params=_params(("parallel", "arbitrary")),
        name=name,
    )(*args)


def _rms_kernel(x_ref, g_ref, o_ref):
    x = x_ref[...]
    ms = jnp.mean(x * x, axis=-1, keepdims=True)
    o_ref[...] = x * lax.rsqrt(ms + EPS) * g_ref[...]


def _rmsnorm_rows(x, g, tm=256):
    m, d = x.shape
    tm = _pick(m, tm)
    return pl.pallas_call(
        _rms_kernel,
        grid=(m // tm,),
        in_specs=[pl.BlockSpec((tm, d), lambda i: (i, 0)), pl.BlockSpec((1, d), lambda i: (0, 0))],
        out_specs=pl.BlockSpec((tm, d), lambda i: (i, 0)),
        out_shape=jax.ShapeDtypeStruct((m, d), F32),
        compiler_params=_params(("parallel",)),
        name="final_norm",
    )(x, g.reshape(1, d))


def _rope_table(pos):
    half = ROPE_DIM // 2
    freqs = jnp.exp(-math.log(ROPE_BASE) * jnp.arange(half, dtype=F32) / half)
    ang = pos.astype(F32)[:, None] * freqs[None, :]
    cos, sin = jnp.cos(ang), jnp.sin(ang)
    return jnp.concatenate([cos, cos, -sin, sin], axis=-1)


def _rope_block(blk, table):
    t = blk * table
    return t + pltpu.roll(t, ROPE_DIM, 1)


def _att_post_kernel(ckv_ref, kr_ref, g_ref, cs_ref, lat_ref, krr_ref, krp_ref):
    c = ckv_ref[...]
    ms = jnp.mean(c * c, axis=-1, keepdims=True)
    lat_ref[...] = c * lax.rsqrt(ms + EPS) * g_ref[...]
    rr = _rope_block(kr_ref[...], cs_ref[...])
    krr_ref[...] = rr
    lane = lax.broadcasted_iota(I32, rr.shape, 1)
    krp_ref[...] = jnp.where(lane < ROPE_DIM, rr, 0.0).astype(BF16)


def _att_post(z, g_kv, cs, tm=512):
    t = z.shape[0]
    tm = _pick(t, tm)
    return pl.pallas_call(
        _att_post_kernel,
        grid=(t // tm,),
        in_specs=[pl.BlockSpec((tm, KV_RANK), lambda i: (i, Q_RANK // KV_RANK)),
                  pl.BlockSpec((tm, LANES), lambda i: (i, (Q_RANK + KV_RANK) // LANES)),
                  pl.BlockSpec((1, KV_RANK), lambda i: (0, 0)),
                  pl.BlockSpec((tm, LANES), lambda i: (i, 0))],
        out_specs=[pl.BlockSpec((tm, KV_RANK), lambda i: (i, 0)),
                   pl.BlockSpec((tm, LANES), lambda i: (i, 0)),
                   pl.BlockSpec((tm, LANES), lambda i: (i, 0))],
        out_shape=[jax.ShapeDtypeStruct((t, KV_RANK), F32),
                   jax.ShapeDtypeStruct((t, LANES), F32),
                   jax.ShapeDtypeStruct((t, LANES), BF16)],
        compiler_params=_params(("parallel",)),
        name="att_post",
    )(z, z, g_kv.reshape(1, KV_RANK), cs)


def _mla_kernel(qn_ref, qr_ref, cs_ref, kv_ref, kr_ref, o_ref, q_s, m_s, acc_s, *, scale, tq, hp):
    i = pl.program_id(1)
    cs = cs_ref[...]
    for h in range(hp):
        lo, hi = h * LANES, (h + 1) * LANES
        qn = qn_ref[:, lo:hi] * scale
        qr = _rope_block(qr_ref[:, lo:hi], cs) * scale
        q_s[h] = jnp.concatenate([qn, qr], axis=1).astype(BF16)
        m_s[h] = jnp.full((tq, 1), NEG_INF, F32)
        acc_s[h] = jnp.zeros((tq, 2 * LANES), F32)

    ones = jnp.ones((tq, LANES), BF16)

    def tile(j, masked):
        off = pl.multiple_of(j * tq, tq)
        kr = kr_ref[pl.ds(off, tq), :]
        for h in range(hp):
            kn = kv_ref[pl.ds(off, tq), 2 * h * LANES:(2 * h + 1) * LANES]
            v = kv_ref[pl.ds(off, tq), (2 * h + 1) * LANES:(2 * h + 2) * LANES]
            s = _dot_nt(q_s[h], jnp.concatenate([kn, kr], axis=1))
            if masked:
                row = lax.broadcasted_iota(I32, (tq, tq), 0) // CHUNK
                col = lax.broadcasted_iota(I32, (tq, tq), 1) // CHUNK
                s = jnp.where(col <= row, s, NEG_INF)
            m_old = m_s[h]
            m_new = jnp.maximum(m_old, jnp.max(s, axis=-1, keepdims=True))
            alpha = jnp.exp(m_old - m_new)
            p = jnp.exp(s - m_new).astype(BF16)
            acc_s[h] = alpha * acc_s[h] + _dot(p, jnp.concatenate([v, ones], axis=1))
            m_s[h] = m_new

    def body(j, c):
        tile(j, False)
        return c

    lax.fori_loop(0, i, body, 0)
    tile(i, True)
    for h in range(hp):
        acc = acc_s[h]
        o_ref[:, h * LANES:(h + 1) * LANES] = (acc[:, :LANES] / acc[:, LANES:]).astype(o_ref.dtype)


def _mla_prompt(q, cs, kv, krp, seq, tq=512, hp=2):
    nq = seq // tq
    ng = MLA_HEADS // hp
    kern = functools.partial(_mla_kernel, scale=(NOPE_DIM + ROPE_DIM) ** -0.5, tq=tq, hp=hp)
    return pl.pallas_call(
        kern,
        grid=(ng, nq),
        in_specs=[pl.BlockSpec((tq, hp * LANES), lambda g, i: (i, g)),
                  pl.BlockSpec((tq, hp * LANES), lambda g, i: (i, ng + g)),
                  pl.BlockSpec((tq, LANES), lambda g, i: (i, 0)),
                  pl.BlockSpec((seq, 2 * hp * LANES), lambda g, i: (0, g)),
                  pl.BlockSpec((seq, LANES), lambda g, i: (0, 0))],
        out_specs=pl.BlockSpec((tq, hp * LANES), lambda g, i: (i, g)),
        out_shape=jax.ShapeDtypeStruct((seq, MLA_HEADS * MLA_VDIM), BF16),
        scratch_shapes=[pltpu.VMEM((hp, tq, 2 * LANES), BF16), pltpu.VMEM((hp, tq, 1), F32),
                        pltpu.VMEM((hp, tq, 2 * LANES), F32)],
        compiler_params=_params(("parallel", "arbitrary")),
        name="mla_prompt",
    )(q, q, cs, kv, krp)


def _band_kernel(q_ref, k0_ref, k1_ref, k2_ref, v0_ref, v1_ref, v2_ref, b_ref, o_ref, *, scale, tq, hp):
    i = pl.program_id(1)
    k_refs = (k0_ref, k1_ref, k2_ref)
    v_refs = (v0_ref, v1_ref, v2_ref)
    for h in range(hp):
        lo, hi = h * LANES, (h + 1) * LANES
        q = (q_ref[:, lo:hi] * scale).astype(BF16)
        ss = []
        for n in range(3):
            s = _dot_nt(q, k_refs[n][:, lo:hi].astype(BF16)) + b_ref[h, :, n * tq:(n + 1) * tq]
            if n < 2:
                s = jnp.where(i + n >= 2, s, NEG_INF)
            ss.append(s)
        m = jnp.max(ss[0], axis=-1, keepdims=True)
        for s in ss[1:]:
            m = jnp.maximum(m, jnp.max(s, axis=-1, keepdims=True))
        l = jnp.zeros((tq, 1), F32)
        acc = jnp.zeros((tq, LANES), F32)
        for n in range(3):
            p = jnp.exp(ss[n] - m)
            l = l + jnp.sum(p, axis=-1, keepdims=True)
            acc = acc + _dot(p.astype(BF16), v_refs[n][:, lo:hi].astype(BF16))
        o_ref[:, lo:hi] = (acc / l).astype(o_ref.dtype)


def _band_prompt(z, bias, seq):
    tq = BAND_TQ
    nq = seq // tq
    hp = 2
    ng = BAND_HEADS // hp
    wb = hp * LANES
    qb0 = 2048 // wb
    kb0 = 4096 // wb
    vb0 = 6144 // wb
    kern = functools.partial(_band_kernel, scale=BAND_HDIM ** -0.5, tq=tq, hp=hp)

    def kv_spec(col0, n):
        return pl.BlockSpec((tq, wb), lambda g, i: (jnp.maximum(i + n - 2, 0), col0 + g))

    return pl.pallas_call(
        kern,
        grid=(ng, nq),
        in_specs=[pl.BlockSpec((tq, wb), lambda g, i: (i, qb0 + g)),
                  kv_spec(kb0, 0), kv_spec(kb0, 1), kv_spec(kb0, 2),
                  kv_spec(vb0, 0), kv_spec(vb0, 1), kv_spec(vb0, 2),
                  pl.BlockSpec((hp, tq, 3 * tq), lambda g, i: (g, 0, 0))],
        out_specs=pl.BlockSpec((tq, wb), lambda g, i: (i, g)),
        out_shape=jax.ShapeDtypeStruct((seq, BAND_HEADS * BAND_HDIM), BF16),
        compiler_params=_params(("parallel", "arbitrary")),
        name="band_prompt",
    )(z, z, z, z, z, z, z, bias)


def _bias_kernel(row_ref, bp_ref, bs_ref, *, tq, wk, sq, sk):
    w = row_ref.shape[-1]
    t = jnp.broadcast_to(row_ref[0], (tq, w))
    t = pltpu.roll(t, 0, 1, stride=1, stride_axis=0)
    bs_ref[0] = t[:sq, :sk]
    tp = t[:, :wk]
    rc = lax.broadcasted_iota(I32, (tq, wk), 0) // CHUNK
    cc = lax.broadcasted_iota(I32, (tq, wk), 1) // CHUNK
    ok = (cc >= rc) & (cc <= rc + BAND_PAST_CHUNKS)
    bp_ref[0] = jnp.where(ok, tp, NEG_INF)


def _band_bias(table, sq, sk):
    tq = BAND_TQ
    wk = 3 * tq
    w = 4 * tq
    h = table.shape[0]
    back = 2 * tq
    n_hi = back - REL_MAX + 1
    mid = table[:, 1:N_REL - 1][:, ::-1]
    n_lo = wk + 1 - n_hi - mid.shape[1]
    row = jnp.concatenate([jnp.broadcast_to(table[:, -1:], (h, n_hi)), mid,
                           jnp.broadcast_to(table[:, :1], (h, n_lo)),
                           jnp.broadcast_to(table[:, -1:], (h, w - wk - 1))], axis=1)
    kern = functools.partial(_bias_kernel, tq=tq, wk=wk, sq=sq, sk=sk)
    return pl.pallas_call(
        kern,
        grid=(h,),
        in_specs=[pl.BlockSpec((1, 1, w), lambda i: (i, 0, 0))],
        out_specs=[pl.BlockSpec((1, tq, wk), lambda i: (i, 0, 0)),
                   pl.BlockSpec((1, sq, sk), lambda i: (i, 0, 0))],
        out_shape=[jax.ShapeDtypeStruct((h, tq, wk), F32), jax.ShapeDtypeStruct((h, sq, sk), F32)],
        compiler_params=_params(("parallel",)),
        name="band_bias",
    )(row.reshape(h, 1, w))


def _attn1_kernel(*refs, has_rope, has_bias, scale):
    it = iter(refs)
    q_ref = next(it)
    if has_rope:
        qr_ref = next(it)
        cs_ref = next(it)
    k_ref = next(it)
    if has_rope:
        kr_ref = next(it)
    v_ref = next(it)
    if has_bias:
        b_ref = next(it)
    o_ref = next(it)
    q = (q_ref[0].astype(F32) * scale).astype(BF16)
    s = _dot_nt(q, k_ref[0].astype(BF16))
    if has_rope:
        q2 = (_rope_block(qr_ref[0], cs_ref[0]) * scale).astype(BF16)
        s = s + _dot_nt(q2, kr_ref[0].astype(BF16))
    if has_bias:
        s = s + b_ref[0]
    m = jnp.max(s, axis=-1, keepdims=True)
    p = jnp.exp(s - m)
    l = jnp.sum(p, axis=-1, keepdims=True)
    o = _dot(p.astype(BF16), v_ref[0].astype(BF16)) / l
    o_ref[0] = o.astype(o_ref.dtype)


def _attn1(q, k, v, *, heads, tq, q0, k0, v0, scale, qr=None, qr0=0, cs=None, kr=None, bias=None, name):
    g, lq, _ = q.shape
    lk = k.shape[1]
    nq = lq // tq
    kq0, kstep = k0
    vq0, vstep = v0
    in_specs = [pl.BlockSpec((1, tq, LANES), lambda b, h, i: (b, i, q0 + h))]
    args = [q]
    if qr is not None:
        in_specs += [pl.BlockSpec((1, tq, LANES), lambda b, h, i: (b, i, qr0 + h)),
                     pl.BlockSpec((1, tq, LANES), lambda b, h, i: (b, i, 0))]
        args += [qr, cs]
    in_specs.append(pl.BlockSpec((1, lk, LANES), lambda b, h, i: (b, 0, kq0 + kstep * h)))
    args.append(k)
    if kr is not None:
        in_specs.append(pl.BlockSpec((1, lk, LANES), lambda b, h, i: (b, 0, 0)))
        args.append(kr)
    in_specs.append(pl.BlockSpec((1, lk, LANES), lambda b, h, i: (b, 0, vq0 + vstep * h)))
    args.append(v)
    if bias is not None:
        in_specs.append(pl.BlockSpec((1, tq, lk), lambda b, h, i: (h, 0, 0)))
        args.append(bias)
    kern = functools.partial(_attn1_kernel, has_rope=qr is not None, has_bias=bias is not None, scale=scale)
    return pl.pallas_call(
        kern,
        grid=(g, heads, nq),
        in_specs=in_specs,
        out_specs=pl.BlockSpec((1, tq, LANES), lambda b, h, i: (b, i, h)),
        out_shape=jax.ShapeDtypeStruct((g, lq, heads * LANES), BF16),
        compiler_params=_params(("parallel", "parallel", "parallel")),
        name=name,
    )(*args)


def _gelu_tanh(y):
    return 0.5 * y * (1.0 + jnp.tanh(0.7978845608028654 * (y + 0.044715 * y * y * y)))


def _s5_kernel(u_ref, wb_ref, wc_ref, cst_ref, d_ref, h0r_ref, h0i_ref, y_ref, hrf_ref, hif_ref,
               bu_s, cr_s, ci_s, *, tm, sw):
    i = pl.program_id(2)

    @pl.when(i == 0)
    def _():
        cr_s[...] = h0r_ref[0, 0]
        ci_s[...] = h0i_ref[0, 0]

    u = u_ref[...]
    bu_s[...] = _dot(u.astype(BF16), wb_ref[0])

    def cst(kk):
        return cst_ref[0, kk * SUBLANES:(kk + 1) * SUBLANES, :]

    def body(r, carry):
        cr, ci = carry
        off = pl.multiple_of(r * SUBLANES, SUBLANES)
        xr = bu_s[pl.ds(off, SUBLANES), 0:sw]
        xi = bu_s[pl.ds(off, SUBLANES), sw:2 * sw]
        for n, d in enumerate((1, 2, 4)):
            ar, ai = cst(2 * n), cst(2 * n + 1)
            sr = pltpu.roll(xr, d, 0)
            si = pltpu.roll(xi, d, 0)
            xr, xi = xr + ar * sr - ai * si, xi + ar * si + ai * sr
        pr, pi_ = cst(6), cst(7)
        xr, xi = xr + pr * cr - pi_ * ci, xi + pr * ci + pi_ * cr
        bu_s[pl.ds(off, SUBLANES), 0:sw] = xr
        bu_s[pl.ds(off, SUBLANES), sw:2 * sw] = xi
        cr = jnp.broadcast_to(xr[SUBLANES - 1:SUBLANES, :], (SUBLANES, sw))
        ci = jnp.broadcast_to(xi[SUBLANES - 1:SUBLANES, :], (SUBLANES, sw))
        return cr, ci

    cr, ci = lax.fori_loop(0, tm // SUBLANES, body, (cr_s[...], ci_s[...]))
    cr_s[...] = cr
    ci_s[...] = ci
    hrf_ref[0, 0] = cr
    hif_ref[0, 0] = ci
    y = _dot(bu_s[...].astype(BF16), wc_ref[0]) + d_ref[...] * u
    y_ref[...] = _gelu_tanh(y)


def _s5_prep(lam_re, lam_im, log_dt, b_re, b_im, c_re, c_im):
    g, p = lam_re.shape
    gs = S5_SLAB_GROUPS
    ns = g // gs
    dt = jnp.exp(log_dt.astype(F32))[:, None]
    lr, li = lam_re.astype(F32), lam_im.astype(F32)
    mag = jnp.exp(lr * dt)
    ab_re, ab_im = mag * jnp.cos(li * dt), mag * jnp.sin(li * dt)
    nr, ni = ab_re - 1.0, ab_im
    den = lr * lr + li * li
    cf_re = (nr * lr + ni * li) / den
    cf_im = (ni * lr - nr * li) / den
    br, bi = b_re.astype(F32), b_im.astype(F32)
    bb_re = cf_re[..., None] * br - cf_im[..., None] * bi
    bb_im = cf_re[..., None] * bi + cf_im[..., None] * br
    eye = jnp.eye(gs, dtype=F32)

    def bdiag_in(bb):
        t = jnp.einsum('sgpc,gh->sgchp', bb.reshape(ns, gs, p, S5_GROUP), eye)
        return t.reshape(ns, gs * S5_GROUP, gs * p)

    def bdiag_out(cc):
        t = jnp.einsum('sgcp,gh->sgphc', cc.reshape(ns, gs, S5_GROUP, p), eye)
        return t.reshape(ns, gs * p, gs * S5_GROUP)

    wb = jnp.concatenate([bdiag_in(bb_re), bdiag_in(bb_im)], axis=-1).astype(BF16)
    wc = jnp.concatenate([bdiag_out(c_re.astype(F32)), bdiag_out(-c_im.astype(F32))], axis=1).astype(BF16)
    ar = ab_re.reshape(ns, gs * p)
    ai = ab_im.reshape(ns, gs * p)
    pw = [(ar, ai)]
    for _ in range(SUBLANES - 1):
        qr_, qi_ = pw[-1]
        pw.append((qr_ * ar - qi_ * ai, qr_ * ai + qi_ * ar))
    rows = jnp.arange(SUBLANES)[None, :, None]

    def masked(v, d):
        return jnp.where(rows >= d, v[:, None, :], 0.0)

    cst = []
    for d in (1, 2, 4):
        cst += [masked(pw[d - 1][0], d), masked(pw[d - 1][1], d)]
    cst.append(jnp.stack([pw[t][0] for t in range(SUBLANES)], axis=1))
    cst.append(jnp.stack([pw[t][1] for t in range(SUBLANES)], axis=1))
    cst = jnp.concatenate(cst, axis=1)
    return wb, wc, cst


def _s5(z, prep, d_skip, h0r, h0i, *, batch, length, row0, tm):
    wb, wc, cst = prep
    ns = wb.shape[0]
    sw = wb.shape[2] // 2
    nt = length // tm
    rb = row0 // tm
    kern = functools.partial(_s5_kernel, tm=tm, sw=sw)
    st_spec = pl.BlockSpec((1, 1, SUBLANES, sw), lambda b, s, i: (b, s, 0, 0))
    y, hrf, hif = pl.pallas_call(
        kern,
        grid=(batch, ns, nt),
        in_specs=[pl.BlockSpec((tm, LANES), lambda b, s, i: (rb + b * nt + i, s)),
                  pl.BlockSpec((1, LANES, 2 * sw), lambda b, s, i: (s, 0, 0)),
                  pl.BlockSpec((1, 2 * sw, LANES), lambda b, s, i: (s, 0, 0)),
                  pl.BlockSpec((1, 8 * SUBLANES, sw), lambda b, s, i: (s, 0, 0)),
                  pl.BlockSpec((1, LANES), lambda b, s, i: (0, s)),
                  st_spec, st_spec],
        out_specs=[pl.BlockSpec((tm, LANES), lambda b, s, i: (b * nt + i, s)), st_spec, st_spec],
        out_shape=[jax.ShapeDtypeStruct((batch * length, ns * LANES), F32),
                   jax.ShapeDtypeStruct((batch, ns, SUBLANES, sw), F32),
                   jax.ShapeDtypeStruct((batch, ns, SUBLANES, sw), F32)],
        scratch_shapes=[pltpu.VMEM((tm, 2 * sw), F32), pltpu.VMEM((SUBLANES, sw), F32),
                        pltpu.VMEM((SUBLANES, sw), F32)],
        compiler_params=_params(("parallel", "parallel", "arbitrary")),
        name="s5_scan",
    )(z, wb, wc, cst, d_skip.reshape(1, -1).astype(F32), h0r, h0i)
    return y, hrf[:, :, 0, :], hif[:, :, 0, :]


def _s5_state_in(h0, ns):
    b = h0.shape[0]
    s = h0.astype(F32).reshape(b, ns, 1, -1)
    return jnp.broadcast_to(s, (b, ns, SUBLANES, s.shape[-1]))


def _hgrn_levels(c):
    ii = np.arange(c)[:, None]
    jj = np.arange(c)[None, :]
    x = ii ^ jj
    lev = np.where(ii > jj, np.floor(np.log2(np.maximum(x, 1))).astype(np.int32), -1)
    lev = np.where(ii == jj, 100, lev)
    return jnp.asarray(lev, dtype=I32)


def _hgrn_kernel(qz_ref, fz_ref, vz_ref, gz_ref, lb_ref, gn_ref, s0_ref, lev_ref, o_ref, sf_ref,
                 st_s, g_s, *, c):
    step = pl.program_id(2)

    @pl.when(step == 0)
    def _():
        st_s[...] = s0_ref[0, 0]

    lb = lb_ref[...]
    fz = fz_ref[...]
    qz = qz_ref[...]
    gz = gz_ref[...]
    v = vz_ref[...]
    logf = jnp.log(lb + (1.0 - lb) * _sigmoid(fz))
    k = (1.0 - lb) * _sigmoid(-fz)
    q = qz * _sigmoid(qz)

    ri = lax.broadcasted_iota(I32, (c, c), 0)
    ci = lax.broadcasted_iota(I32, (c, c), 1)
    tri = jnp.where(ci <= ri, 1.0, 0.0).astype(BF16)
    hi = logf.astype(BF16)
    r1 = logf - hi.astype(F32)
    mid = r1.astype(BF16)
    lo = (r1 - mid.astype(F32)).astype(BF16)
    gcum = _dot(tri, hi) + _dot(tri, mid) + _dot(tri, lo)
    g_s[...] = gcum

    sub = lax.broadcasted_iota(I32, (c, LANES), 0)
    lev = lev_ref[...]
    qb = q.astype(BF16)
    kb = k.astype(BF16)
    a = jnp.where(lev == 100, _dot_nt(qb, kb), 0.0)
    nlev = int(math.log2(c))
    for l in range(nlev):
        m = 1 << l
        if 2 * m <= SUBLANES:
            pos = sub & (2 * m - 1)
            ref = gcum
            for off in range(-(m - 1), m + 1):
                if off == 0:
                    continue
                cand = pltpu.roll(gcum, off % c, 0)
                ref = jnp.where(pos == (m - 1 + off), cand, ref)
        else:
            blocks = []
            for b0 in range(0, c, 2 * m):
                r = b0 + m - 1
                blocks.append(jnp.broadcast_to(g_s[r:r + 1, :], (2 * m, LANES)))
            ref = blocks[0] if len(blocks) == 1 else jnp.concatenate(blocks, axis=0)
        f = jnp.exp(-jnp.abs(gcum - ref))
        pl_ = _dot_nt((q * f).astype(BF16), (k * f).astype(BF16))
        a = jnp.where(lev == l, pl_, a)

    st = st_s[...]
    o = _dot_nt((q * jnp.exp(gcum)).astype(BF16), st.astype(BF16)) + _dot(a.astype(BF16), v.astype(BF16))
    glast = g_s[c - 1:c, :]
    kdec = k * jnp.exp(glast - gcum)
    st_new = st * jnp.exp(glast) + _dot(v.T.astype(BF16), kdec.astype(BF16))
    st_s[...] = st_new
    sf_ref[0, 0] = st_new
    ms = jnp.mean(o * o, axis=-1, keepdims=True)
    on = o * lax.rsqrt(ms + EPS) * gn_ref[...]
    o_ref[...] = (on * (gz * _sigmoid(gz))).astype(o_ref.dtype)


def _hgrn(z, lb, g_hg, s0t, *, batch, length, row0, c):
    nc = length // c
    rb = row0 // c
    hw = HG_HEADS
    kern = functools.partial(_hgrn_kernel, c=c)

    def zspec(col0):
        return pl.BlockSpec((c, LANES), lambda b, h, i: (rb + b * nc + i, col0 + h))

    st_spec = pl.BlockSpec((1, 1, HG_VDIM, HG_KDIM), lambda b, h, i: (b, h, 0, 0))
    return pl.pallas_call(
        kern,
        grid=(batch, hw, nc),
        in_specs=[zspec(hw), zspec(2 * hw), zspec(3 * hw), zspec(4 * hw),
                  pl.BlockSpec((1, LANES), lambda b, h, i: (0, h)),
                  pl.BlockSpec((1, LANES), lambda b, h, i: (0, 0)),
                  st_spec,
                  pl.BlockSpec((c, c), lambda b, h, i: (0, 0))],
        out_specs=[pl.BlockSpec((c, LANES), lambda b, h, i: (b * nc + i, h)), st_spec],
        out_shape=[jax.ShapeDtypeStruct((batch * length, hw * HG_VDIM), BF16),
                   jax.ShapeDtypeStruct((batch, hw, HG_VDIM, HG_KDIM), F32)],
        scratch_shapes=[pltpu.VMEM((HG_VDIM, HG_KDIM), F32), pltpu.VMEM((c, LANES), F32)],
        compiler_params=_params(("parallel", "parallel", "arbitrary")),
        name="hgrn2",
    )(z, z, z, z, lb.reshape(1, -1), g_hg.reshape(1, -1), s0t, _hgrn_levels(c))


def _router_kernel(x_ref, g_ref, wh_ref, b_ref, xp_ref, idx_ref, wt_ref, cnt_ref, cnt_s,
                   *, tm, n_exp):
    @pl.when(pl.program_id(0) == 0)
    def _():
        cnt_s[...] = jnp.zeros(cnt_s.shape, F32)

    x = x_ref[...]
    ms = jnp.mean(x * x, axis=-1, keepdims=True)
    xn = x * lax.rsqrt(ms + EPS) * g_ref[...]
    xh = xn.astype(BF16)
    xhf = xh.astype(F32)
    bits = lax.bitcast_convert_type(xhf, U32)
    half = x.shape[1] // 2
    xp_ref[...] = (bits[:, :half] >> 16) | (bits[:, half:] & jnp.uint32(0xFFFF0000))
    logits = _dot(xh, wh_ref[...]) + b_ref[...]
    lane = lax.broadcasted_iota(I32, (tm, LANES), 1)
    lanef = lane.astype(F32)
    neg = jnp.float32(-jnp.inf)
    l = jnp.where(lane < n_exp, logits, neg)
    vals, sels, idxs = [], [], []
    for _ in range(TOP_K):
        mx = jnp.max(l, axis=-1, keepdims=True)
        ix = jnp.min(jnp.where(l == mx, lanef, float(LANES)), axis=-1, keepdims=True)
        sk = lanef == ix
        l = jnp.where(sk, neg, l)
        vals.append(mx)
        sels.append(sk)
        idxs.append(ix.astype(I32))
    es = [jnp.exp(v - vals[0]) for v in vals]
    den = es[0]
    for e in es[1:]:
        den = den + e
    sel = jnp.zeros((tm, LANES), F32)
    for sk in sels:
        sel = jnp.where(sk, 1.0, sel)
    ri = lax.broadcasted_iota(I32, (tm, tm), 0)
    ci = lax.broadcasted_iota(I32, (tm, tm), 1)
    tri = jnp.where(ci < ri, 1.0, 0.0).astype(BF16)
    rank = _dot(tri, sel.astype(BF16)) + cnt_s[...]
    cnt = cnt_s[...] + jnp.sum(sel, axis=0, keepdims=True)
    cnt_s[...] = cnt
    cnt_ref[...] = jnp.broadcast_to(cnt, cnt_ref.shape).astype(I32)
    io = jnp.zeros((tm, LANES), I32)
    wo = jnp.zeros((tm, LANES), F32)
    for kk in range(TOP_K):
        rk = jnp.sum(jnp.where(sels[kk], rank, 0.0), axis=-1, keepdims=True).astype(I32)
        io = jnp.where(lane == kk, idxs[kk], io)
        io = jnp.where(lane == TOP_K + kk, rk, io)
        wo = jnp.where(lane == kk, es[kk] / den, wo)
    idx_ref[...] = io
    wt_ref[...] = wo


def _router(x, g, w_router, b_router, tm=256):
    t, d = x.shape
    n_exp = w_router.shape[1]
    tm = _pick(t, tm)
    wpad = jnp.zeros((d, LANES), F32).at[:, :n_exp].set(w_router.astype(F32))
    wh = wpad.astype(BF16)
    bpad = jnp.zeros((1, LANES), F32).at[0, :n_exp].set(b_router.astype(F32))
    kern = functools.partial(_router_kernel, tm=tm, n_exp=n_exp)
    return pl.pallas_call(
        kern,
        grid=(t // tm,),
        in_specs=[pl.BlockSpec((tm, d), lambda i: (i, 0)),
                  pl.BlockSpec((1, d), lambda i: (0, 0)),
                  pl.BlockSpec((d, LANES), lambda i: (0, 0)),
                  pl.BlockSpec((1, LANES), lambda i: (0, 0))],
        out_specs=[pl.BlockSpec((tm, d // 2), lambda i: (i, 0)),
                   pl.BlockSpec((tm, LANES), lambda i: (i, 0)),
                   pl.BlockSpec((tm, LANES), lambda i: (i, 0)),
                   pl.BlockSpec((SUBLANES, LANES), lambda i: (0, 0))],
        out_shape=[jax.ShapeDtypeStruct((t, d // 2), U32),
                   jax.ShapeDtypeStruct((t, LANES), I32),
                   jax.ShapeDtypeStruct((t, LANES), F32),
                   jax.ShapeDtypeStruct((SUBLANES, LANES), I32)],
        scratch_shapes=[pltpu.VMEM((1, LANES), F32)],
        compiler_params=_params(("arbitrary",)),
        name="moe_router",
    )(x, g.reshape(1, d), wh, bpad)


def _dispatch_kernel(slot_ref, x_ref, zero_ref, xs_ref, sem, *, tm):
    del zero_ref

    def row_copy(r, kk):
        return pltpu.make_async_copy(x_ref.at[pl.ds(r, 1)],
                                     xs_ref.at[pl.ds(slot_ref[r * TOP_K + kk], 1)], sem)

    def start(r, c):
        for kk in range(TOP_K):
            row_copy(r, kk).start()
        return c

    def wait(r, c):
        for kk in range(TOP_K):
            row_copy(r, kk).wait()
        return c

    lax.fori_loop(0, tm, start, 0)
    lax.fori_loop(0, tm, wait, 0)


def _dispatch(xp, slots, n_rows, tm=256):
    t, dh = xp.shape
    tm = _pick(t, tm)
    kern = functools.partial(_dispatch_kernel, tm=tm)
    return pl.pallas_call(
        kern,
        grid=(t // tm,),
        in_specs=[pl.BlockSpec((tm * TOP_K,), lambda i: (i,), memory_space=pltpu.SMEM),
                  pl.BlockSpec((tm, dh), lambda i: (i, 0)),
                  pl.BlockSpec(memory_space=pl.ANY)],
        out_specs=pl.BlockSpec(memory_space=pl.ANY),
        out_shape=jax.ShapeDtypeStruct((n_rows, dh), U32),
        scratch_shapes=[pltpu.SemaphoreType.DMA],
        input_output_aliases={2: 0},
        compiler_params=pltpu.CompilerParams(dimension_semantics=("arbitrary",), vmem_limit_bytes=VMEM_LIMIT,
                                             has_side_effects=True),
        name="moe_dispatch",
    )(slots.reshape(-1), xp, jnp.zeros((n_rows, dh), U32))


def _unpack(u):
    lo = lax.bitcast_convert_type(u << 16, F32).astype(BF16)
    hi = lax.bitcast_convert_type(u & jnp.uint32(0xFFFF0000), F32).astype(BF16)
    return lo, hi


def _ffn1_kernel(te_ref, nu_ref, xs_ref, wg_ref, wl_ref, bg_ref, bl_ref, a_ref, wg_s, wl_s):
    t = pl.program_id(1)
    prev = te_ref[jnp.maximum(t - 1, 0)]
    changed = (t == 0) | (te_ref[t] != prev)

    @pl.when(changed)
    def _():
        wg_s[...] = wg_ref[...].astype(BF16)
        wl_s[...] = wl_ref[...].astype(BF16)

    @pl.when(t < nu_ref[0])
    def _():
        lo, hi = _unpack(xs_ref[...])
        half = lo.shape[1]
        hg = _dot(lo, wg_s[:half, :]) + _dot(hi, wg_s[half:, :]) + bg_ref[...]
        hl = _dot(lo, wl_s[:half, :]) + _dot(hi, wl_s[half:, :]) + bl_ref[...]
        xg = jnp.minimum(hg, SWIGLU_LIMIT)
        xl = jnp.clip(hl, -SWIGLU_LIMIT, SWIGLU_LIMIT)
        a_ref[...] = (xg * _sigmoid(SWIGLU_ALPHA * xg) * (xl + 1.0)).astype(a_ref.dtype)

    @pl.when(t >= nu_ref[0])
    def _():
        a_ref[...] = jnp.zeros(a_ref.shape, a_ref.dtype)


def _ffn1(xs, tile_e, n_used, w_e1, b_e1, layer, tm, nc=256):
    n_rows, dh = xs.shape
    d = 2 * dh
    f = w_e1.shape[-1] // 2
    nc = _pick(f, nc)
    nj = f // nc
    n_tiles = n_rows // tm
    b1 = b_e1.reshape(b_e1.shape[0], b_e1.shape[1], 1, 2 * f)

    def tcl(t, nu):
        return jnp.minimum(t, nu[0] - 1)

    return pl.pallas_call(
        _ffn1_kernel,
        grid_spec=pltpu.PrefetchScalarGridSpec(
            num_scalar_prefetch=2,
            grid=(nj, n_tiles),
            in_specs=[pl.BlockSpec((tm, dh), lambda j, t, te, nu: (tcl(t, nu), 0)),
                      pl.BlockSpec((None, None, d, nc), lambda j, t, te, nu: (layer, te[t], 0, j)),
                      pl.BlockSpec((None, None, d, nc), lambda j, t, te, nu: (layer, te[t], 0, nj + j)),
                      pl.BlockSpec((None, None, 1, nc), lambda j, t, te, nu: (layer, te[t], 0, j)),
                      pl.BlockSpec((None, None, 1, nc), lambda j, t, te, nu: (layer, te[t], 0, nj + j))],
            out_specs=pl.BlockSpec((tm, nc), lambda j, t, te, nu: (t, j)),
            scratch_shapes=[pltpu.VMEM((d, nc), BF16), pltpu.VMEM((d, nc), BF16)]),
        out_shape=jax.ShapeDtypeStruct((n_rows, f), BF16),
        compiler_params=_params(("arbitrary", "arbitrary")),
        name="moe_ffn1",
    )(tile_e, n_used, xs, w_e1, w_e1, b1, b1)


def _ffn2_kernel(te_ref, nu_ref, a_ref, w_ref, b_ref, o_ref, w_s):
    t = pl.program_id(1)
    prev = te_ref[jnp.maximum(t - 1, 0)]
    changed = (t == 0) | (te_ref[t] != prev)

    @pl.when(changed)
    def _():
        w_s[...] = w_ref[...].astype(BF16)

    @pl.when(t < nu_ref[0])
    def _():
        o_ref[...] = _dot(a_ref[...], w_s[...]) + b_ref[...]

    @pl.when(t >= nu_ref[0])
    def _():
        o_ref[...] = jnp.zeros(o_ref.shape, o_ref.dtype)


def _ffn2(a, tile_e, n_used, w_e2, b_e2, layer, tm, nc=2048):
    n_rows, f = a.shape
    d = w_e2.shape[-1]
    nc = _pick(d, nc)
    nj = d // nc
    n_tiles = n_rows // tm
    b2 = b_e2.reshape(b_e2.shape[0], b_e2.shape[1], 1, d)

    def tcl(t, nu):
        return jnp.minimum(t, nu[0] - 1)

    return pl.pallas_call(
        _ffn2_kernel,
        grid_spec=pltpu.PrefetchScalarGridSpec(
            num_scalar_prefetch=2,
            grid=(nj, n_tiles),
            in_specs=[pl.BlockSpec((tm, f), lambda j, t, te, nu: (tcl(t, nu), 0)),
                      pl.BlockSpec((None, None, f, nc), lambda j, t, te, nu: (layer, te[t], 0, j)),
                      pl.BlockSpec((None, None, 1, nc), lambda j, t, te, nu: (layer, te[t], 0, j))],
            out_specs=pl.BlockSpec((tm, nc), lambda j, t, te, nu: (t, j)),
            scratch_shapes=[pltpu.VMEM((f, nc), BF16)]),
        out_shape=jax.ShapeDtypeStruct((n_rows, d), F32),
        compiler_params=_params(("arbitrary", "arbitrary")),
        name="moe_ffn2",
    )(tile_e, n_used, a, w_e2, b2)


def _combine_kernel(slot_ref, x_ref, w_ref, o_hbm, y_ref, buf, sem, *, tm):
    def row_copy(r, kk):
        return pltpu.make_async_copy(o_hbm.at[pl.ds(slot_ref[r * TOP_K + kk], 1)],
                                     buf.at[kk, pl.ds(r, 1)], sem)

    def start(r, c):
        for kk in range(TOP_K):
            row_copy(r, kk).start()
        return c

    def wait(r, c):
        for kk in range(TOP_K):
            row_copy(r, kk).wait()
        return c

    lax.fori_loop(0, tm, start, 0)
    lax.fori_loop(0, tm, wait, 0)
    w = w_ref[...]
    acc = w[:, 0:1] * buf[0]
    for kk in range(1, TOP_K):
        acc = acc + w[:, kk:kk + 1] * buf[kk]
    y_ref[...] = x_ref[...] + acc


def _combine(x, wts, slots, o, tm=128):
    t, d = x.shape
    tm = _pick(t, tm)
    kern = functools.partial(_combine_kernel, tm=tm)
    return pl.pallas_call(
        kern,
        grid=(t // tm,),
        in_specs=[pl.BlockSpec((tm * TOP_K,), lambda i: (i,), memory_space=pltpu.SMEM),
                  pl.BlockSpec((tm, d), lambda i: (i, 0)),
                  pl.BlockSpec((tm, LANES), lambda i: (i, 0)),
                  pl.BlockSpec(memory_space=pl.ANY)],
        out_specs=pl.BlockSpec((tm, d), lambda i: (i, 0)),
        out_shape=jax.ShapeDtypeStruct((t, d), F32),
        scratch_shapes=[pltpu.VMEM((TOP_K, tm, d), F32), pltpu.SemaphoreType.DMA],
        compiler_params=_params(("arbitrary",)),
        name="moe_combine",
    )(slots.reshape(-1), x, wts, o)


def _moe(x, g, w_router, b_router, w_e1, b_e1, w_e2, b_e2, layer):
    t = x.shape[0]
    n_exp = w_router.shape[1]
    tm = MOE_TM
    xp, idxrank, wts, cnt = _router(x, g, w_router, b_router)
    idx = idxrank[:, :TOP_K]
    rank = idxrank[:, TOP_K:2 * TOP_K]
    cnt = cnt[0, :n_exp]
    ptiles = (cnt + tm - 1) // tm
    tend = jnp.cumsum(ptiles)
    pstart = (tend - ptiles) * tm
    onehot = idx[:, :, None] == jnp.arange(n_exp, dtype=I32)[None, None, :]
    slots = (jnp.sum(jnp.where(onehot, pstart[None, None, :], 0), axis=-1) + rank).astype(I32)
    n_tiles = (t * TOP_K + n_exp * (tm - 1)) // tm
    n_used = tend[-1:].astype(I32)
    tid = jnp.minimum(jnp.arange(n_tiles, dtype=I32), n_used[0] - 1)
    tile_e = jnp.minimum(jnp.sum(tid[:, None] >= tend[None, :], axis=1), n_exp - 1).astype(I32)
    xs = _dispatch(xp, slots, n_tiles * tm)
    a = _ffn1(xs, tile_e, n_used, w_e1, b_e1, layer, tm)
    o = _ffn2(a, tile_e, n_used, w_e2, b_e2, layer, tm)
    return _combine(x, wts, slots, o)


def _att_weights(w_in, w_uq):
    d = w_in.shape[0]
    bw = BAND_HEADS * BAND_HDIM
    o1, o2, o3 = Q_RANK, Q_RANK + KV_RANK, Q_RANK + KV_RANK + ROPE_DIM
    half = ROPE_DIM // 2
    kr = w_in[:, o2:o3]
    krs = jnp.concatenate([kr[:, half:], kr[:, :half]], axis=1)
    pad = jnp.zeros((d, 2048 - (o3 + ROPE_DIM)), w_in.dtype)
    w_in2 = jnp.concatenate([w_in[:, :o3], krs, pad, w_in[:, o3:o3 + 3 * bw]], axis=1).astype(BF16)
    w3 = w_uq.reshape(Q_RANK, MLA_HEADS, NOPE_DIM + ROPE_DIM)
    r = w3[:, :, NOPE_DIM:]
    rs = jnp.concatenate([r[:, :, half:], r[:, :, :half]], axis=-1)
    w_uq2 = jnp.concatenate([w3[:, :, :NOPE_DIM].reshape(Q_RANK, -1),
                             jnp.concatenate([r, rs], axis=-1).reshape(Q_RANK, -1)], axis=1).astype(BF16)
    return w_in2, w_uq2


def _att_layer(x, seq, nb, ls, past, cs, c_lat, c_kr, c_bk, c_bv, g_mix, w_in, g_q, w_uq, g_kv, w_ukv,
               rel_bias, w_out):
    ts = nb * ls
    keep = c_bk.shape[1]
    assert past % CHUNK == 0 and ls <= CHUNK and keep == BAND_PAST_CHUNKS * CHUNK and past >= keep
    assert 2 * BAND_TQ == BAND_PAST_CHUNKS * CHUNK and seq % 512 == 0
    w_in2, w_uq2 = _att_weights(w_in, w_uq)
    z = _mm(x, w_in2, gain=g_mix, name="att_in")
    lat, krr, krp = _att_post(z, g_kv, cs)
    q = _mm(z, w_uq2, gain=g_q, xcol=0, name="att_uq")
    w_ukv2 = w_ukv.astype(BF16)
    kv_p = _mm(lat, w_ukv2, rows=seq, out_dtype=BF16, name="att_ukv_p")
    o_a = _mla_prompt(q, cs, kv_p, krp, seq)
    bias_p, bias_s = _band_bias(rel_bias, ls, keep + ls)
    o_b = _band_prompt(z, bias_p, seq)
    lat_s = lat[seq:].reshape(nb, ls, KV_RANK)
    lat_all = jnp.concatenate([c_lat.astype(F32), lat_s], axis=1)
    nk = past + ls
    kv_s = _mm(lat_all.reshape(nb * nk, KV_RANK), w_ukv2, out_dtype=BF16, tm=528, name="att_ukv_s")
    kv_s = kv_s.reshape(nb, nk, -1)
    c_krp = jnp.concatenate([c_kr.astype(F32), jnp.zeros(c_kr.shape[:-1] + (LANES - ROPE_DIM,), F32)], axis=-1)
    kr_all = jnp.concatenate([c_krp.astype(BF16), krp[seq:].reshape(nb, ls, LANES)], axis=1)
    q_s = q[seq:].reshape(nb, ls, -1)
    cs_s = cs[seq:].reshape(nb, ls, LANES)
    o_as = _attn1(q_s, kv_s, kv_s, heads=MLA_HEADS, tq=ls, q0=0, k0=(0, 2), v0=(1, 2),
                  scale=(NOPE_DIM + ROPE_DIM) ** -0.5, qr=q_s, qr0=MLA_HEADS, cs=cs_s, kr=kr_all,
                  name="mla_sample")
    z_s = z[seq:].reshape(nb, ls, -1)
    bw = BAND_HEADS * BAND_HDIM
    kb_all = jnp.concatenate([c_bk.reshape(nb, keep, bw).astype(F32), z_s[:, :, 4096:4096 + bw]], axis=1)
    vb_all = jnp.concatenate([c_bv.reshape(nb, keep, bw).astype(F32), z_s[:, :, 6144:6144 + bw]], axis=1)
    o_bs = _attn1(z_s, kb_all, vb_all, heads=BAND_HEADS, tq=ls, q0=2048 // LANES, k0=(0, 1), v0=(0, 1),
                  scale=BAND_HDIM ** -0.5, bias=bias_s, name="band_sample")
    o = jnp.concatenate([jnp.concatenate([o_a, o_b], axis=1),
                         jnp.concatenate([o_as.reshape(ts, -1), o_bs.reshape(ts, -1)], axis=1)], axis=0)
    x = _mm(o, w_out.astype(BF16), residual=x, name="att_out")
    outs = dict(
        lat_p=lat[:seq], kr_p=krr[:seq, :ROPE_DIM],
        bk_p=z[seq - min(keep, seq):seq, 4096:4096 + bw], bv_p=z[seq - min(keep, seq):seq, 6144:6144 + bw],
        lat_s=lat_s, kr_s=krr[seq:, :ROPE_DIM].reshape(nb, ls, ROPE_DIM),
        bk_s=z_s[:, :, 4096:4096 + bw], bv_s=z_s[:, :, 6144:6144 + bw])
    return x, outs


def _rec_layer(x, seq, nb, ls, s5r, s5i, hg0, lb, g_mix, w_in, lam_re, lam_im, log_dt, b_re, b_im, c_re, c_im,
               d_skip, w_glu, g_hg, w_out):
    z = _mm(x, w_in.astype(BF16), gain=g_mix, name="rec_in")
    prep = _s5_prep(lam_re, lam_im, log_dt, b_re, b_im, c_re, c_im)
    ns = prep[0].shape[0]
    g, p = lam_re.shape
    zeros_p = jnp.zeros((1, ns, SUBLANES, prep[0].shape[2] // 2), F32)
    y_p, hr_p, hi_p = _s5(z, prep, d_skip, zeros_p, zeros_p, batch=1, length=seq, row0=0, tm=_pick(seq, 256))
    y_s, hr_s, hi_s = _s5(z, prep, d_skip, _s5_state_in(s5r, ns), _s5_state_in(s5i, ns),
                          batch=nb, length=ls, row0=seq, tm=ls)
    y = jnp.concatenate([y_p, y_s], axis=0)
    o_c = _mm(y, w_glu.astype(BF16), glu=y, out_dtype=BF16, name="rec_glu")
    hz = jnp.zeros((1, HG_HEADS, HG_VDIM, HG_KDIM), F32)
    od_p, sf_p = _hgrn(z, lb, g_hg, hz, batch=1, length=seq, row0=0, c=_pick(seq, 256))
    od_s, sf_s = _hgrn(z, lb, g_hg, jnp.swapaxes(hg0.astype(F32), -1, -2), batch=nb, length=ls, row0=seq, c=ls)
    o = jnp.concatenate([o_c, jnp.concatenate([od_p, od_s], axis=0)], axis=1)
    x = _mm(o, w_out.astype(BF16), residual=x, name="rec_out")
    outs = dict(sr_p=hr_p.reshape(1, g, p), si_p=hi_p.reshape(1, g, p), hg_p=jnp.swapaxes(sf_p, -1, -2),
                sr_s=hr_s.reshape(nb, g, p), si_s=hi_s.reshape(nb, g, p), hg_s=jnp.swapaxes(sf_s, -1, -2))
    return x, outs


def _mem_layer(x, seq, nb, ls, mem, cmk, cmv, g_x, g_m, w_mq, w_mkv, w_mo):
    n_mem = mem.shape[0]
    mw = MEM_HEADS * MEM_HDIM
    kvm = _mm(mem, w_mkv.astype(BF16), gain=g_m, tm=n_mem, name="mem_kv")
    q = _mm(x, w_mq.astype(BF16), gain=g_x, out_dtype=BF16, name="mem_q")
    scale = MEM_HDIM ** -0.5
    kvm3 = kvm.reshape(1, n_mem, 2 * mw)
    o_p = _attn1(q[:seq].reshape(1, seq, mw), kvm3, kvm3, heads=MEM_HEADS, tq=_pick(seq, 512), q0=0,
                 k0=(0, 1), v0=(MEM_HEADS, 1), scale=scale, name="mem_attn_p")
    o_s = _attn1(q[seq:].reshape(nb, ls, mw), cmk.reshape(nb, n_mem, mw), cmv.reshape(nb, n_mem, mw),
                 heads=MEM_HEADS, tq=ls, q0=0, k0=(0, 1), v0=(0, 1), scale=scale, name="mem_attn_s")
    o = jnp.concatenate([o_p.reshape(seq, mw), o_s.reshape(nb * ls, mw)], axis=0)
    x = _mm(o, w_mo.astype(BF16), residual=x, name="mem_out")
    mk = kvm[:, :mw].reshape(1, n_mem, MEM_HEADS, MEM_HDIM)
    mv = kvm[:, mw:].reshape(1, n_mem, MEM_HEADS, MEM_HDIM)
    return x, mk, mv


def kernel(x_prompt, x_sample, cache_mla_latent, cache_mla_krope, cache_band_k, cache_band_v, cache_mem_k,
           cache_mem_v, state_s5_re, state_s5_im, state_hgrn, mem_prompt, g_mix, g_xattn, g_mem, g_moe,
           g_final, w_in_att, g_q, w_uq, g_kv, w_ukv, rel_bias, w_out_att, w_in_rec, s5_lam_re, s5_lam_im,
           s5_log_dt, s5_b_re, s5_b_im, s5_c_re, s5_c_im, s5_d, w_glu, hg_lb_logits, g_hg, w_out_rec, w_mq,
           w_mkv, w_mo, w_router, b_router, w_e1, b_e1, w_e2, b_e2):
    bp, seq, d = x_prompt.shape
    nb, ls, _ = x_sample.shape
    assert bp == 1
    past = cache_mla_latent.shape[2]
    depth = g_mix.shape[0]
    x = jnp.concatenate([x_prompt.reshape(seq, d), x_sample.reshape(nb * ls, d)], axis=0).astype(F32)
    pos = jnp.concatenate([jnp.arange(seq), jnp.tile(past + jnp.arange(ls), nb)])
    cs = _rope_table(pos)
    probs = jax.nn.softmax(hg_lb_logits.astype(F32), axis=0)
    lb_all = jnp.cumsum(probs, axis=0) - probs
    att, rec, mks, mvs = [], [], [], []
    for l in range(depth):
        j = l // 2
        if l % 2 == 0:
            x, o = _att_layer(x, seq, nb, ls, past, cs, cache_mla_latent[j], cache_mla_krope[j],
                              cache_band_k[j], cache_band_v[j], g_mix[l], w_in_att[j], g_q[j], w_uq[j],
                              g_kv[j], w_ukv[j], rel_bias[j], w_out_att[j])
            att.append(o)
        else:
            x, o = _rec_layer(x, seq, nb, ls, state_s5_re[j], state_s5_im[j], state_hgrn[j], lb_all[j],
                              g_mix[l], w_in_rec[j], s5_lam_re[j], s5_lam_im[j], s5_log_dt[j], s5_b_re[j],
                              s5_b_im[j], s5_c_re[j], s5_c_im[j], s5_d[j], w_glu[j], g_hg[j], w_out_rec[j])
            rec.append(o)
        x, mk, mv = _mem_layer(x, seq, nb, ls, mem_prompt[0], cache_mem_k[l], cache_mem_v[l], g_xattn[l],
                               g_mem[l], w_mq[l], w_mkv[l], w_mo[l])
        mks.append(mk)
        mvs.append(mv)
        x = _moe(x, g_moe[l], w_router[l], b_router[l], w_e1, b_e1, w_e2, b_e2, l)
    y = _rmsnorm_rows(x, g_final)

    def st(lst, key, shape):
        return jnp.stack([o[key].reshape(shape) for o in lst], axis=0)

    keep_p = min(BAND_PAST_CHUNKS * CHUNK, seq)
    bshape_p = (1, keep_p, BAND_HEADS, BAND_HDIM)
    bshape_s = (nb, ls, BAND_HEADS, BAND_HDIM)
    gsz, psz = s5_lam_re.shape[1:]
    hshape = (HG_HEADS, HG_KDIM, HG_VDIM)
    return (y[:seq].reshape(1, seq, d), y[seq:].reshape(nb, ls, d),
            st(att, "lat_p", (1, seq, KV_RANK)), st(att, "kr_p", (1, seq, ROPE_DIM)),
            st(att, "bk_p", bshape_p), st(att, "bv_p", bshape_p),
            jnp.stack(mks, 0), jnp.stack(mvs, 0),
            st(rec, "sr_p", (1, gsz, psz)), st(rec, "si_p", (1, gsz, psz)), st(rec, "hg_p", (1,) + hshape),
            st(att, "lat_s", (nb, ls, KV_RANK)), st(att, "kr_s", (nb, ls, ROPE_DIM)),
            st(att, "bk_s", bshape_s), st(att, "bv_s", bshape_s),
            st(rec, "sr_s", (nb, gsz, psz)), st(rec, "si_s", (nb, gsz, psz)), st(rec, "hg_s", (nb,) + hshape))
```

```python
import functools
import math

import numpy as np
import jax
import jax.numpy as jnp
from jax import lax
from jax.experimental import pallas as pl
from jax.experimental.pallas import tpu as pltpu

F32 = jnp.float32
BF16 = jnp.bfloat16
I32 = jnp.int32
U32 = jnp.uint32

CHUNK = 64
MLA_HEADS = 16
Q_RANK = 1024
KV_RANK = 512
NOPE_DIM = 128
ROPE_DIM = 64
MLA_VDIM = 128
ROPE_BASE = 10000.0
BAND_HEADS = 16
BAND_HDIM = 128
BAND_PAST_CHUNKS = 8
REL_MAX = 256
N_REL = REL_MAX + CHUNK
S5_GROUP = 16
S5_STATE = 64
HG_HEADS = 16
HG_KDIM = 128
HG_VDIM = 128
MEM_HEADS = 4
MEM_HDIM = 128
TOP_K = 4
SWIGLU_LIMIT = 7.0
SWIGLU_ALPHA = 1.702
EPS = 1e-6
NEG_INF = -1e30

LANES = 128
SUBLANES = 8
VMEM_LIMIT = 56 * 1024 * 1024
S5_SLAB_GROUPS = 8
MOE_TM = 512
BAND_TQ = 256


def _params(sem):
    return pltpu.CompilerParams(dimension_semantics=sem, vmem_limit_bytes=VMEM_LIMIT)


def _dot(a, b):
    return jnp.dot(a, b, preferred_element_type=F32)


def _dot_nt(a, b):
    return lax.dot_general(a, b, (((1,), (1,)), ((), ())), preferred_element_type=F32)


def _sigmoid(x):
    return 1.0 / (1.0 + jnp.exp(-x))


def _pick(n, pref):
    t = min(pref, n)
    while n % t:
        t //= 2
    return t


def _mm_kernel(*refs, has_gain, has_bias, has_res, has_glu):
    it = iter(refs)
    x_ref = next(it)
    w_ref = next(it)
    g_ref = next(it) if has_gain else None
    b_ref = next(it) if has_bias else None
    r_ref = next(it) if has_res else None
    y_ref = next(it) if has_glu else None
    o_ref = next(it)
    xs_ref = next(it)

    @pl.when(pl.program_id(1) == 0)
    def _():
        x = x_ref[...].astype(F32)
        if has_gain:
            ms = jnp.mean(x * x, axis=-1, keepdims=True)
            x = x * lax.rsqrt(ms + EPS) * g_ref[...]
        xs_ref[...] = x.astype(BF16)

    acc = _dot(xs_ref[...], w_ref[...])
    if has_bias:
        acc = acc + b_ref[...]
    if has_glu:
        acc = y_ref[...].astype(F32) * _sigmoid(acc)
    if has_res:
        acc = acc + r_ref[...]
    o_ref[...] = acc.astype(o_ref.dtype)


def _mm(x, w, *, gain=None, bias=None, residual=None, glu=None, xcol=0, rows=None, row0=0,
        tm=512, tn=1024, out_dtype=F32, name="mm"):
    k, n = w.shape
    m = x.shape[0] if rows is None else rows
    tm = _pick(m, tm)
    tn = _pick(n, tn)
    assert row0 % tm == 0 and x.shape[1] % k == 0
    rb = row0 // tm
    in_specs = [pl.BlockSpec((tm, k), lambda i, j: (i + rb, xcol)),
                pl.BlockSpec((k, tn), lambda i, j: (0, j))]
    args = [x, w]
    if gain is not None:
        in_specs.append(pl.BlockSpec((1, k), lambda i, j: (0, 0)))
        args.append(gain.reshape(1, k).astype(F32))
    if bias is not None:
        in_specs.append(pl.BlockSpec((1, tn), lambda i, j: (0, j)))
        args.append(bias.reshape(1, n).astype(F32))
    if residual is not None:
        in_specs.append(pl.BlockSpec((tm, tn), lambda i, j: (i, j)))
        args.append(residual)
    if glu is not None:
        in_specs.append(pl.BlockSpec((tm, tn), lambda i, j: (i, j)))
        args.append(glu)
    kern = functools.partial(_mm_kernel, has_gain=gain is not None, has_bias=bias is not None,
                             has_res=residual is not None, has_glu=glu is not None)
    return pl.pallas_call(
        kern,
        grid=(m // tm, n // tn),
        in_specs=in_specs,
        out_specs=pl.BlockSpec((tm, tn), lambda i, j: (i, j)),
        out_shape=jax.ShapeDtypeStruct((m, n), out_dtype),
        scratch_shapes=[pltpu.VMEM((tm, k), BF16)],
        compiler_params=_params(("parallel", "arbitrary")),
        name=name,
    )(*args)


def _rms_kernel(x_ref, g_ref, o_ref):
    x = x_ref[...]
    ms = jnp.mean(x * x, axis=-1, keepdims=True)
    o_ref[...] = x * lax.rsqrt(ms + EPS) * g_ref[...]


def _rmsnorm_rows(x, g, tm=256):
    m, d = x.shape
    tm = _pick(m, tm)
    return pl.pallas_call(
        _rms_kernel,
        grid=(m // tm,),
        in_specs=[pl.BlockSpec((tm, d), lambda i: (i, 0)), pl.BlockSpec((1, d), lambda i: (0, 0))],
        out_specs=pl.BlockSpec((tm, d), lambda i: (i, 0)),
        out_shape=jax.ShapeDtypeStruct((m, d), F32),
        compiler_params=_params(("parallel",)),
        name="final_norm",
    )(x, g.reshape(1, d))


def _rope_table(pos):
    half = ROPE_DIM // 2
    freqs = jnp.exp(-math.log(ROPE_BASE) * jnp.arange(half, dtype=F32) / half)
    ang = pos.astype(F32)[:, None] * freqs[None, :]
    cos, sin = jnp.cos(ang), jnp.sin(ang)
    return jnp.concatenate([cos, cos, -sin, sin], axis=-1)


def _rope_block(blk, table):
    t = blk * table
    return t + pltpu.roll(t, ROPE_DIM, 1)


def _att_post_kernel(ckv_ref, kr_ref, g_ref, cs_ref, lat_ref, krr_ref, krp_ref):
    c = ckv_ref[...]
    ms = jnp.mean(c * c, axis=-1, keepdims=True)
    lat_ref[...] = c * lax.rsqrt(ms + EPS) * g_ref[...]
    rr = _rope_block(kr_ref[...], cs_ref[...])
    krr_ref[...] = rr
    lane = lax.broadcasted_iota(I32, rr.shape, 1)
    krp_ref[...] = jnp.where(lane < ROPE_DIM, rr, 0.0).astype(BF16)


def _att_post(z, g_kv, cs, tm=512):
    t = z.shape[0]
    tm = _pick(t, tm)
    return pl.pallas_call(
        _att_post_kernel,
        grid=(t // tm,),
        in_specs=[pl.BlockSpec((tm, KV_RANK), lambda i: (i, Q_RANK // KV_RANK)),
                  pl.BlockSpec((tm, LANES), lambda i: (i, (Q_RANK + KV_RANK) // LANES)),
                  pl.BlockSpec((1, KV_RANK), lambda i: (0, 0)),
                  pl.BlockSpec((tm, LANES), lambda i: (i, 0))],
        out_specs=[pl.BlockSpec((tm, KV_RANK), lambda i: (i, 0)),
                   pl.BlockSpec((tm, LANES), lambda i: (i, 0)),
                   pl.BlockSpec((tm, LANES), lambda i: (i, 0))],
        out_shape=[jax.ShapeDtypeStruct((t, KV_RANK), F32),
                   jax.ShapeDtypeStruct((t, LANES), F32),
                   jax.ShapeDtypeStruct((t, LANES), BF16)],
        compiler_params=_params(("parallel",)),
        name="att_post",
    )(z, z, g_kv.reshape(1, KV_RANK), cs)


def _mla_kernel(qn_ref, qr_ref, cs_ref, kv_ref, kr_ref, o_ref, q_s, m_s, acc_s, *, scale, tq, hp):
    i = pl.program_id(1)
    cs = cs_ref[...]
    for h in range(hp):
        lo, hi = h * LANES, (h + 1) * LANES
        qn = qn_ref[:, lo:hi] * scale
        qr = _rope_block(qr_ref[:, lo:hi], cs) * scale
        q_s[h] = jnp.concatenate([qn, qr], axis=1).astype(BF16)
        m_s[h] = jnp.full((tq, 1), NEG_INF, F32)
        acc_s[h] = jnp.zeros((tq, 2 * LANES), F32)

    ones = jnp.ones((tq, LANES), BF16)

    def tile(j, masked):
        off = pl.multiple_of(j * tq, tq)
        kr = kr_ref[pl.ds(off, tq), :]
        for h in range(hp):
            kn = kv_ref[pl.ds(off, tq), 2 * h * LANES:(2 * h + 1) * LANES]
            v = kv_ref[pl.ds(off, tq), (2 * h + 1) * LANES:(2 * h + 2) * LANES]
            s = _dot_nt(q_s[h], jnp.concatenate([kn, kr], axis=1))
            if masked:
                row = lax.broadcasted_iota(I32, (tq, tq), 0) // CHUNK
                col = lax.broadcasted_iota(I32, (tq, tq), 1) // CHUNK
                s = jnp.where(col <= row, s, NEG_INF)
            m_old = m_s[h]
            m_new = jnp.maximum(m_old, jnp.max(s, axis=-1, keepdims=True))
            alpha = jnp.exp(m_old - m_new)
            p = jnp.exp(s - m_new).astype(BF16)
            acc_s[h] = alpha * acc_s[h] + _dot(p, jnp.concatenate([v, ones], axis=1))
            m_s[h] = m_new

    def body(j, c):
        tile(j, False)
        return c

    lax.fori_loop(0, i, body, 0)
    tile(i, True)
    for h in range(hp):
        acc = acc_s[h]
        o_ref[:, h * LANES:(h + 1) * LANES] = (acc[:, :LANES] / acc[:, LANES:]).astype(o_ref.dtype)


def _mla_prompt(q, cs, kv, krp, seq, tq=512, hp=4):
    nq = seq // tq
    ng = MLA_HEADS // hp
    kern = functools.partial(_mla_kernel, scale=(NOPE_DIM + ROPE_DIM) ** -0.5, tq=tq, hp=hp)
    return pl.pallas_call(
        kern,
        grid=(ng, nq),
        in_specs=[pl.BlockSpec((tq, hp * LANES), lambda g, i: (i, g)),
                  pl.BlockSpec((tq, hp * LANES), lambda g, i: (i, ng + g)),
                  pl.BlockSpec((tq, LANES), lambda g, i: (i, 0)),
                  pl.BlockSpec((seq, 2 * hp * LANES), lambda g, i: (0, g)),
                  pl.BlockSpec((seq, LANES), lambda g, i: (0, 0))],
        out_specs=pl.BlockSpec((tq, hp * LANES), lambda g, i: (i, g)),
        out_shape=jax.ShapeDtypeStruct((seq, MLA_HEADS * MLA_VDIM), BF16),
        scratch_shapes=[pltpu.VMEM((hp, tq, 2 * LANES), BF16), pltpu.VMEM((hp, tq, 1), F32),
                        pltpu.VMEM((hp, tq, 2 * LANES), F32)],
        compiler_params=_params(("parallel", "arbitrary")),
        name="mla_prompt",
    )(q, q, cs, kv, krp)


def _band_kernel(q_ref, k0_ref, k1_ref, k2_ref, v0_ref, v1_ref, v2_ref, b_ref, o_ref, *, scale, tq, hp):
    i = pl.program_id(1)
    k_refs = (k0_ref, k1_ref, k2_ref)
    v_refs = (v0_ref, v1_ref, v2_ref)
    for h in range(hp):
        lo, hi = h * LANES, (h + 1) * LANES
        q = (q_ref[:, lo:hi] * scale).astype(BF16)
        ss = []
        for n in range(3):
            s = _dot_nt(q, k_refs[n][:, lo:hi].astype(BF16)) + b_ref[h, :, n * tq:(n + 1) * tq]
            if n < 2:
                s = jnp.where(i + n >= 2, s, NEG_INF)
            ss.append(s)
        m = jnp.max(ss[0], axis=-1, keepdims=True)
        for s in ss[1:]:
            m = jnp.maximum(m, jnp.max(s, axis=-1, keepdims=True))
        l = jnp.zeros((tq, 1), F32)
        acc = jnp.zeros((tq, LANES), F32)
        for n in range(3):
            p = jnp.exp(ss[n] - m)
            l = l + jnp.sum(p, axis=-1, keepdims=True)
            acc = acc + _dot(p.astype(BF16), v_refs[n][:, lo:hi].astype(BF16))
        o_ref[:, lo:hi] = (acc / l).astype(o_ref.dtype)


def _band_prompt(z, bias, seq):
    tq = BAND_TQ
    nq = seq // tq
    hp = 2
    ng = BAND_HEADS // hp
    wb = hp * LANES
    qb0 = 2048 // wb
    kb0 = 4096 // wb
    vb0 = 6144 // wb
    kern = functools.partial(_band_kernel, scale=BAND_HDIM ** -0.5, tq=tq, hp=hp)

    def kv_spec(col0, n):
        return pl.BlockSpec((tq, wb), lambda g, i: (jnp.maximum(i + n - 2, 0), col0 + g))

    return pl.pallas_call(
        kern,
        grid=(ng, nq),
        in_specs=[pl.BlockSpec((tq, wb), lambda g, i: (i, qb0 + g)),
                  kv_spec(kb0, 0), kv_spec(kb0, 1), kv_spec(kb0, 2),
                  kv_spec(vb0, 0), kv_spec(vb0, 1), kv_spec(vb0, 2),
                  pl.BlockSpec((hp, tq, 3 * tq), lambda g, i: (g, 0, 0))],
        out_specs=pl.BlockSpec((tq, wb), lambda g, i: (i, g)),
        out_shape=jax.ShapeDtypeStruct((seq, BAND_HEADS * BAND_HDIM), BF16),
        compiler_params=_params(("parallel", "arbitrary")),
        name="band_prompt",
    )(z, z, z, z, z, z, z, bias)


def _bias_kernel(row_ref, bp_ref, bs_ref, *, tq, wk, sq, sk):
    w = row_ref.shape[-1]
    t = jnp.broadcast_to(row_ref[0], (tq, w))
    t = pltpu.roll(t, 0, 1, stride=1, stride_axis=0)
    bs_ref[0] = t[:sq, :sk]
    tp = t[:, :wk]
    rc = lax.broadcasted_iota(I32, (tq, wk), 0) // CHUNK
    cc = lax.broadcasted_iota(I32, (tq, wk), 1) // CHUNK
    ok = (cc >= rc) & (cc <= rc + BAND_PAST_CHUNKS)
    bp_ref[0] = jnp.where(ok, tp, NEG_INF)


def _band_bias(table, sq, sk):
    tq = BAND_TQ
    wk = 3 * tq
    w = 4 * tq
    h = table.shape[0]
    back = 2 * tq
    n_hi = back - REL_MAX + 1
    mid = table[:, 1:N_REL - 1][:, ::-1]
    n_lo = wk + 1 - n_hi - mid.shape[1]
    row = jnp.concatenate([jnp.broadcast_to(table[:, -1:], (h, n_hi)), mid,
                           jnp.broadcast_to(table[:, :1], (h, n_lo)),
                           jnp.broadcast_to(table[:, -1:], (h, w - wk - 1))], axis=1)
    kern = functools.partial(_bias_kernel, tq=tq, wk=wk, sq=sq, sk=sk)
    return pl.pallas_call(
        kern,
        grid=(h,),
        in_specs=[pl.BlockSpec((1, 1, w), lambda i: (i, 0, 0))],
        out_specs=[pl.BlockSpec((1, tq, wk), lambda i: (i, 0, 0)),
                   pl.BlockSpec((1, sq, sk), lambda i: (i, 0, 0))],
        out_shape=[jax.ShapeDtypeStruct((h, tq, wk), F32), jax.ShapeDtypeStruct((h, sq, sk), F32)],
        compiler_params=_params(("parallel",)),
        name="band_bias",
    )(row.reshape(h, 1, w))


def _attn1_kernel(*refs, has_rope, has_bias, scale):
    it = iter(refs)
    q_ref = next(it)
    if has_rope:
        qr_ref = next(it)
        cs_ref = next(it)
    k_ref = next(it)
    if has_rope:
        kr_ref = next(it)
    v_ref = next(it)
    if has_bias:
        b_ref = next(it)
    o_ref = next(it)
    q = (q_ref[0].astype(F32) * scale).astype(BF16)
    s = _dot_nt(q, k_ref[0].astype(BF16))
    if has_rope:
        q2 = (_rope_block(qr_ref[0], cs_ref[0]) * scale).astype(BF16)
        s = s + _dot_nt(q2, kr_ref[0].astype(BF16))
    if has_bias:
        s = s + b_ref[0]
    m = jnp.max(s, axis=-1, keepdims=True)
    p = jnp.exp(s - m)
    l = jnp.sum(p, axis=-1, keepdims=True)
    o = _dot(p.astype(BF16), v_ref[0].astype(BF16)) / l
    o_ref[0] = o.astype(o_ref.dtype)


def _attn1(q, k, v, *, heads, tq, q0, k0, v0, scale, qr=None, qr0=0, cs=None, kr=None, bias=None, name):
    g, lq, _ = q.shape
    lk = k.shape[1]
    nq = lq // tq
    kq0, kstep = k0
    vq0, vstep = v0
    in_specs = [pl.BlockSpec((1, tq, LANES), lambda b, h, i: (b, i, q0 + h))]
    args = [q]
    if qr is not None:
        in_specs += [pl.BlockSpec((1, tq, LANES), lambda b, h, i: (b, i, qr0 + h)),
                     pl.BlockSpec((1, tq, LANES), lambda b, h, i: (b, i, 0))]
        args += [qr, cs]
    in_specs.append(pl.BlockSpec((1, lk, LANES), lambda b, h, i: (b, 0, kq0 + kstep * h)))
    args.append(k)
    if kr is not None:
        in_specs.append(pl.BlockSpec((1, lk, LANES), lambda b, h, i: (b, 0, 0)))
        args.append(kr)
    in_specs.append(pl.BlockSpec((1, lk, LANES), lambda b, h, i: (b, 0, vq0 + vstep * h)))
    args.append(v)
    if bias is not None:
        in_specs.append(pl.BlockSpec((1, tq, lk), lambda b, h, i: (h, 0, 0)))
        args.append(bias)
    kern = functools.partial(_attn1_kernel, has_rope=qr is not None, has_bias=bias is not None, scale=scale)
    return pl.pallas_call(
        kern,
        grid=(g, heads, nq),
        in_specs=in_specs,
        out_specs=pl.BlockSpec((1, tq, LANES), lambda b, h, i: (b, i, h)),
        out_shape=jax.ShapeDtypeStruct((g, lq, heads * LANES), BF16),
        compiler_params=_params(("parallel", "parallel", "parallel")),
        name=name,
    )(*args)


def _gelu_tanh(y):
    return 0.5 * y * (1.0 + jnp.tanh(0.7978845608028654 * (y + 0.044715 * y * y * y)))


def _s5_kernel(u_ref, wb_ref, wc_ref, cst_ref, d_ref, h0r_ref, h0i_ref, y_ref, hrf_ref, hif_ref,
               bu_s, cr_s, ci_s, *, tm, sw):
    i = pl.program_id(2)

    @pl.when(i == 0)
    def _():
        cr_s[...] = h0r_ref[0, 0]
        ci_s[...] = h0i_ref[0, 0]

    u = u_ref[...]
    bu_s[...] = _dot(u.astype(BF16), wb_ref[0])

    def cst(kk):
        return cst_ref[0, kk * SUBLANES:(kk + 1) * SUBLANES, :]

    def body(r, carry):
        cr, ci = carry
        off = pl.multiple_of(r * SUBLANES, SUBLANES)
        xr = bu_s[pl.ds(off, SUBLANES), 0:sw]
        xi = bu_s[pl.ds(off, SUBLANES), sw:2 * sw]
        for n, d in enumerate((1, 2, 4)):
            ar, ai = cst(2 * n), cst(2 * n + 1)
            sr = pltpu.roll(xr, d, 0)
            si = pltpu.roll(xi, d, 0)
            xr, xi = xr + ar * sr - ai * si, xi + ar * si + ai * sr
        pr, pi_ = cst(6), cst(7)
        xr, xi = xr + pr * cr - pi_ * ci, xi + pr * ci + pi_ * cr
        bu_s[pl.ds(off, SUBLANES), 0:sw] = xr
        bu_s[pl.ds(off, SUBLANES), sw:2 * sw] = xi
        cr = jnp.broadcast_to(xr[SUBLANES - 1:SUBLANES, :], (SUBLANES, sw))
        ci = jnp.broadcast_to(xi[SUBLANES - 1:SUBLANES, :], (SUBLANES, sw))
        return cr, ci

    cr, ci = lax.fori_loop(0, tm // SUBLANES, body, (cr_s[...], ci_s[...]))
    cr_s[...] = cr
    ci_s[...] = ci
    hrf_ref[0, 0] = cr
    hif_ref[0, 0] = ci
    y = _dot(bu_s[...].astype(BF16), wc_ref[0]) + d_ref[...] * u
    y_ref[...] = _gelu_tanh(y)


def _s5_prep(lam_re, lam_im, log_dt, b_re, b_im, c_re, c_im):
    g, p = lam_re.shape
    gs = S5_SLAB_GROUPS
    ns = g // gs
    dt = jnp.exp(log_dt.astype(F32))[:, None]
    lr, li = lam_re.astype(F32), lam_im.astype(F32)
    mag = jnp.exp(lr * dt)
    ab_re, ab_im = mag * jnp.cos(li * dt), mag * jnp.sin(li * dt)
    nr, ni = ab_re - 1.0, ab_im
    den = lr * lr + li * li
    cf_re = (nr * lr + ni * li) / den
    cf_im = (ni * lr - nr * li) / den
    br, bi = b_re.astype(F32), b_im.astype(F32)
    bb_re = cf_re[..., None] * br - cf_im[..., None] * bi
    bb_im = cf_re[..., None] * bi + cf_im[..., None] * br
    eye = jnp.eye(gs, dtype=F32)

    def bdiag_in(bb):
        t = jnp.einsum('sgpc,gh->sgchp', bb.reshape(ns, gs, p, S5_GROUP), eye)
        return t.reshape(ns, gs * S5_GROUP, gs * p)

    def bdiag_out(cc):
        t = jnp.einsum('sgcp,gh->sgphc', cc.reshape(ns, gs, S5_GROUP, p), eye)
        return t.reshape(ns, gs * p, gs * S5_GROUP)

    wb = jnp.concatenate([bdiag_in(bb_re), bdiag_in(bb_im)], axis=-1).astype(BF16)
    wc = jnp.concatenate([bdiag_out(c_re.astype(F32)), bdiag_out(-c_im.astype(F32))], axis=1).astype(BF16)
    ar = ab_re.reshape(ns, gs * p)
    ai = ab_im.reshape(ns, gs * p)
    pw = [(ar, ai)]
    for _ in range(SUBLANES - 1):
        qr_, qi_ = pw[-1]
        pw.append((qr_ * ar - qi_ * ai, qr_ * ai + qi_ * ar))
    rows = jnp.arange(SUBLANES)[None, :, None]

    def masked(v, d):
        return jnp.where(rows >= d, v[:, None, :], 0.0)

    cst = []
    for d in (1, 2, 4):
        cst += [masked(pw[d - 1][0], d), masked(pw[d - 1][1], d)]
    cst.append(jnp.stack([pw[t][0] for t in range(SUBLANES)], axis=1))
    cst.append(jnp.stack([pw[t][1] for t in range(SUBLANES)], axis=1))
    cst = jnp.concatenate(cst, axis=1)
    return wb, wc, cst


def _s5(z, prep, d_skip, h0r, h0i, *, batch, length, row0, tm):
    wb, wc, cst = prep
    ns = wb.shape[0]
    sw = wb.shape[2] // 2
    nt = length // tm
    rb = row0 // tm
    kern = functools.partial(_s5_kernel, tm=tm, sw=sw)
    st_spec = pl.BlockSpec((1, 1, SUBLANES, sw), lambda b, s, i: (b, s, 0, 0))
    y, hrf, hif = pl.pallas_call(
        kern,
        grid=(batch, ns, nt),
        in_specs=[pl.BlockSpec((tm, LANES), lambda b, s, i: (rb + b * nt + i, s)),
                  pl.BlockSpec((1, LANES, 2 * sw), lambda b, s, i: (s, 0, 0)),
                  pl.BlockSpec((1, 2 * sw, LANES), lambda b, s, i: (s, 0, 0)),
                  pl.BlockSpec((1, 8 * SUBLANES, sw), lambda b, s, i: (s, 0, 0)),
                  pl.BlockSpec((1, LANES), lambda b, s, i: (0, s)),
                  st_spec, st_spec],
        out_specs=[pl.BlockSpec((tm, LANES), lambda b, s, i: (b * nt + i, s)), st_spec, st_spec],
        out_shape=[jax.ShapeDtypeStruct((batch * length, ns * LANES), F32),
                   jax.ShapeDtypeStruct((batch, ns, SUBLANES, sw), F32),
                   jax.ShapeDtypeStruct((batch, ns, SUBLANES, sw), F32)],
        scratch_shapes=[pltpu.VMEM((tm, 2 * sw), F32), pltpu.VMEM((SUBLANES, sw), F32),
                        pltpu.VMEM((SUBLANES, sw), F32)],
        compiler_params=_params(("parallel", "parallel", "arbitrary")),
        name="s5_scan",
    )(z, wb, wc, cst, d_skip.reshape(1, -1).astype(F32), h0r, h0i)
    return y, hrf[:, :, 0, :], hif[:, :, 0, :]


def _s5_state_in(h0, ns):
    b = h0.shape[0]
    s = h0.astype(F32).reshape(b, ns, 1, -1)
    return jnp.broadcast_to(s, (b, ns, SUBLANES, s.shape[-1]))


def _hgrn_levels(c):
    ii = np.arange(c)[:, None]
    jj = np.arange(c)[None, :]
    x = ii ^ jj
    lev = np.where(ii > jj, np.floor(np.log2(np.maximum(x, 1))).astype(np.int32), -1)
    lev = np.where(ii == jj, 100, lev)
    return jnp.asarray(lev, dtype=I32)


def _hgrn_kernel(qz_ref, fz_ref, vz_ref, gz_ref, lb_ref, gn_ref, s0_ref, lev_ref, o_ref, sf_ref,
                 st_s, g_s, *, c):
    step = pl.program_id(2)

    @pl.when(step == 0)
    def _():
        st_s[...] = s0_ref[0, 0]

    lb = lb_ref[...]
    fz = fz_ref[...]
    qz = qz_ref[...]
    gz = gz_ref[...]
    v = vz_ref[...]
    logf = jnp.log(lb + (1.0 - lb) * _sigmoid(fz))
    k = (1.0 - lb) * _sigmoid(-fz)
    q = qz * _sigmoid(qz)

    ri = lax.broadcasted_iota(I32, (c, c), 0)
    ci = lax.broadcasted_iota(I32, (c, c), 1)
    tri = jnp.where(ci <= ri, 1.0, 0.0).astype(BF16)
    hi = logf.astype(BF16)
    r1 = logf - hi.astype(F32)
    mid = r1.astype(BF16)
    lo = (r1 - mid.astype(F32)).astype(BF16)
    gcum = _dot(tri, hi) + _dot(tri, mid) + _dot(tri, lo)
    g_s[...] = gcum

    sub = lax.broadcasted_iota(I32, (c, LANES), 0)
    lev = lev_ref[...]
    qb = q.astype(BF16)
    kb = k.astype(BF16)
    a = jnp.where(lev == 100, _dot_nt(qb, kb), 0.0)
    nlev = int(math.log2(c))
    for l in range(nlev):
        m = 1 << l
        if 2 * m <= SUBLANES:
            pos = sub & (2 * m - 1)
            ref = gcum
            for off in range(-(m - 1), m + 1):
                if off == 0:
                    continue
                cand = pltpu.roll(gcum, off % c, 0)
                ref = jnp.where(pos == (m - 1 + off), cand, ref)
        else:
            blocks = []
            for b0 in range(0, c, 2 * m):
                r = b0 + m - 1
                blocks.append(jnp.broadcast_to(g_s[r:r + 1, :], (2 * m, LANES)))
            ref = blocks[0] if len(blocks) == 1 else jnp.concatenate(blocks, axis=0)
        f = jnp.exp(-jnp.abs(gcum - ref))
        pl_ = _dot_nt((q * f).astype(BF16), (k * f).astype(BF16))
        a = jnp.where(lev == l, pl_, a)

    st = st_s[...]
    o = _dot_nt((q * jnp.exp(gcum)).astype(BF16), st.astype(BF16)) + _dot(a.astype(BF16), v.astype(BF16))
    glast = g_s[c - 1:c, :]
    kdec = k * jnp.exp(glast - gcum)
    st_new = st * jnp.exp(glast) + _dot(v.T.astype(BF16), kdec.astype(BF16))
    st_s[...] = st_new
    sf_ref[0, 0] = st_new
    ms = jnp.mean(o * o, axis=-1, keepdims=True)
    on = o * lax.rsqrt(ms + EPS) * gn_ref[...]
    o_ref[...] = (on * (gz * _sigmoid(gz))).astype(o_ref.dtype)


def _hgrn(z, lb, g_hg, s0t, *, batch, length, row0, c):
    nc = length // c
    rb = row0 // c
    hw = HG_HEADS
    kern = functools.partial(_hgrn_kernel, c=c)

    def zspec(col0):
        return pl.BlockSpec((c, LANES), lambda b, h, i: (rb + b * nc + i, col0 + h))

    st_spec = pl.BlockSpec((1, 1, HG_VDIM, HG_KDIM), lambda b, h, i: (b, h, 0, 0))
    return pl.pallas_call(
        kern,
        grid=(batch, hw, nc),
        in_specs=[zspec(hw), zspec(2 * hw), zspec(3 * hw), zspec(4 * hw),
                  pl.BlockSpec((1, LANES), lambda b, h, i: (0, h)),
                  pl.BlockSpec((1, LANES), lambda b, h, i: (0, 0)),
                  st_spec,
                  pl.BlockSpec((c, c), lambda b, h, i: (0, 0))],
        out_specs=[pl.BlockSpec((c, LANES), lambda b, h, i: (b * nc + i, h)), st_spec],
        out_shape=[jax.ShapeDtypeStruct((batch * length, hw * HG_VDIM), BF16),
                   jax.ShapeDtypeStruct((batch, hw, HG_VDIM, HG_KDIM), F32)],
        scratch_shapes=[pltpu.VMEM((HG_VDIM, HG_KDIM), F32), pltpu.VMEM((c, LANES), F32)],
        compiler_params=_params(("parallel", "parallel", "arbitrary")),
        name="hgrn2",
    )(z, z, z, z, lb.reshape(1, -1), g_hg.reshape(1, -1), s0t, _hgrn_levels(c))


def _router_kernel(x_ref, g_ref, wh_ref, b_ref, xp_ref, idx_ref, wt_ref, cnt_ref, cnt_s,
                   *, tm, n_exp):
    @pl.when(pl.program_id(0) == 0)
    def _():
        cnt_s[...] = jnp.zeros(cnt_s.shape, F32)

    x = x_ref[...]
    ms = jnp.mean(x * x, axis=-1, keepdims=True)
    xn = x * lax.rsqrt(ms + EPS) * g_ref[...]
    xh = xn.astype(BF16)
    xhf = xh.astype(F32)
    bits = lax.bitcast_convert_type(xhf, U32)
    half = x.shape[1] // 2
    xp_ref[...] = (bits[:, :half] >> 16) | (bits[:, half:] & jnp.uint32(0xFFFF0000))
    logits = _dot(xh, wh_ref[...]) + b_ref[...]
    lane = lax.broadcasted_iota(I32, (tm, LANES), 1)
    lanef = lane.astype(F32)
    neg = jnp.float32(-jnp.inf)
    l = jnp.where(lane < n_exp, logits, neg)
    vals, sels, idxs = [], [], []
    for _ in range(TOP_K):
        mx = jnp.max(l, axis=-1, keepdims=True)
        ix = jnp.min(jnp.where(l == mx, lanef, float(LANES)), axis=-1, keepdims=True)
        sk = lanef == ix
        l = jnp.where(sk, neg, l)
        vals.append(mx)
        sels.append(sk)
        idxs.append(ix.astype(I32))
    es = [jnp.exp(v - vals[0]) for v in vals]
    den = es[0]
    for e in es[1:]:
        den = den + e
    sel = jnp.zeros((tm, LANES), F32)
    for sk in sels:
        sel = jnp.where(sk, 1.0, sel)
    ri = lax.broadcasted_iota(I32, (tm, tm), 0)
    ci = lax.broadcasted_iota(I32, (tm, tm), 1)
    tri = jnp.where(ci < ri, 1.0, 0.0).astype(BF16)
    rank = _dot(tri, sel.astype(BF16)) + cnt_s[...]
    cnt = cnt_s[...] + jnp.sum(sel, axis=0, keepdims=True)
    cnt_s[...] = cnt
    cnt_ref[...] = jnp.broadcast_to(cnt, cnt_ref.shape).astype(I32)
    io = jnp.zeros((tm, LANES), I32)
    wo = jnp.zeros((tm, LANES), F32)
    for kk in range(TOP_K):
        rk = jnp.sum(jnp.where(sels[kk], rank, 0.0), axis=-1, keepdims=True).astype(I32)
        io = jnp.where(lane == kk, idxs[kk], io)
        io = jnp.where(lane == TOP_K + kk, rk, io)
        wo = jnp.where(lane == kk, es[kk] / den, wo)
    idx_ref[...] = io
    wt_ref[...] = wo


def _router(x, g, w_router, b_router, tm=256):
    t, d = x.shape
    n_exp = w_router.shape[1]
    tm = _pick(t, tm)
    wpad = jnp.zeros((d, LANES), F32).at[:, :n_exp].set(w_router.astype(F32))
    wh = wpad.astype(BF16)
    bpad = jnp.zeros((1, LANES), F32).at[0, :n_exp].set(b_router.astype(F32))
    kern = functools.partial(_router_kernel, tm=tm, n_exp=n_exp)
    return pl.pallas_call(
        kern,
        grid=(t // tm,),
        in_specs=[pl.BlockSpec((tm, d), lambda i: (i, 0)),
                  pl.BlockSpec((1, d), lambda i: (0, 0)),
                  pl.BlockSpec((d, LANES), lambda i: (0, 0)),
                  pl.BlockSpec((1, LANES), lambda i: (0, 0))],
        out_specs=[pl.BlockSpec((tm, d // 2), lambda i: (i, 0)),
                   pl.BlockSpec((tm, LANES), lambda i: (i, 0)),
                   pl.BlockSpec((tm, LANES), lambda i: (i, 0)),
                   pl.BlockSpec((SUBLANES, LANES), lambda i: (0, 0))],
        out_shape=[jax.ShapeDtypeStruct((t, d // 2), U32),
                   jax.ShapeDtypeStruct((t, LANES), I32),
                   jax.ShapeDtypeStruct((t, LANES), F32),
                   jax.ShapeDtypeStruct((SUBLANES, LANES), I32)],
        scratch_shapes=[pltpu.VMEM((1, LANES), F32)],
        compiler_params=_params(("arbitrary",)),
        name="moe_router",
    )(x, g.reshape(1, d), wh, bpad)


def _dispatch_kernel(slot_ref, x_ref, zero_ref, xs_ref, sem, *, tm):
    del zero_ref

    def row_copy(r, kk):
        return pltpu.make_async_copy(x_ref.at[pl.ds(r, 1)],
                                     xs_ref.at[pl.ds(slot_ref[r * TOP_K + kk], 1)], sem)

    def start(r, c):
        for kk in range(TOP_K):
            row_copy(r, kk).start()
        return c

    def wait(r, c):
        for kk in range(TOP_K):
            row_copy(r, kk).wait()
        return c

    lax.fori_loop(0, tm, start, 0)
    lax.fori_loop(0, tm, wait, 0)


def _dispatch(xp, slots, n_rows, tm=512):
    t, dh = xp.shape
    tm = _pick(t, tm)
    kern = functools.partial(_dispatch_kernel, tm=tm)
    return pl.pallas_call(
        kern,
        grid=(t // tm,),
        in_specs=[pl.BlockSpec((tm * TOP_K,), lambda i: (i,), memory_space=pltpu.SMEM),
                  pl.BlockSpec((tm, dh), lambda i: (i, 0)),
                  pl.BlockSpec(memory_space=pl.ANY)],
        out_specs=pl.BlockSpec(memory_space=pl.ANY),
        out_shape=jax.ShapeDtypeStruct((n_rows, dh), U32),
        scratch_shapes=[pltpu.SemaphoreType.DMA],
        input_output_aliases={2: 0},
        compiler_params=pltpu.CompilerParams(dimension_semantics=("arbitrary",), vmem_limit_bytes=VMEM_LIMIT,
                                             has_side_effects=True),
        name="moe_dispatch",
    )(slots.reshape(-1), xp, jnp.zeros((n_rows, dh), U32))


def _unpack(u):
    lo = lax.bitcast_convert_type(u << 16, F32).astype(BF16)
    hi = lax.bitcast_convert_type(u & jnp.uint32(0xFFFF0000), F32).astype(BF16)
    return lo, hi


def _ffn1_kernel(te_ref, nu_ref, xs_ref, wg_ref, wl_ref, bg_ref, bl_ref, a_ref, wg_s, wl_s):
    t = pl.program_id(1)
    prev = te_ref[jnp.maximum(t - 1, 0)]
    changed = (t == 0) | (te_ref[t] != prev)

    @pl.when(changed)
    def _():
        wg_s[...] = wg_ref[...].astype(BF16)
        wl_s[...] = wl_ref[...].astype(BF16)

    @pl.when(t < nu_ref[0])
    def _():
        lo, hi = _unpack(xs_ref[...])
        half = lo.shape[1]
        hg = _dot(lo, wg_s[:half, :]) + _dot(hi, wg_s[half:, :]) + bg_ref[...]
        hl = _dot(lo, wl_s[:half, :]) + _dot(hi, wl_s[half:, :]) + bl_ref[...]
        xg = jnp.minimum(hg, SWIGLU_LIMIT)
        xl = jnp.clip(hl, -SWIGLU_LIMIT, SWIGLU_LIMIT)
        a_ref[...] = (xg * _sigmoid(SWIGLU_ALPHA * xg) * (xl + 1.0)).astype(a_ref.dtype)

    @pl.when(t >= nu_ref[0])
    def _():
        a_ref[...] = jnp.zeros(a_ref.shape, a_ref.dtype)


def _ffn1(xs, tile_e, n_used, w_e1, b_e1, layer, tm, nc=512):
    n_rows, dh = xs.shape
    d = 2 * dh
    f = w_e1.shape[-1] // 2
    nc = _pick(f, nc)
    nj = f // nc
    n_tiles = n_rows // tm
    b1 = b_e1.reshape(b_e1.shape[0], b_e1.shape[1], 1, 2 * f)

    def tcl(t, nu):
        return jnp.minimum(t, nu[0] - 1)

    return pl.pallas_call(
        _ffn1_kernel,
        grid_spec=pltpu.PrefetchScalarGridSpec(
            num_scalar_prefetch=2,
            grid=(nj, n_tiles),
            in_specs=[pl.BlockSpec((tm, dh), lambda j, t, te, nu: (tcl(t, nu), 0)),
                      pl.BlockSpec((None, None, d, nc), lambda j, t, te, nu: (layer, te[t], 0, j)),
                      pl.BlockSpec((None, None, d, nc), lambda j, t, te, nu: (layer, te[t], 0, nj + j)),
                      pl.BlockSpec((None, None, 1, nc), lambda j, t, te, nu: (layer, te[t], 0, j)),
                      pl.BlockSpec((None, None, 1, nc), lambda j, t, te, nu: (layer, te[t], 0, nj + j))],
            out_specs=pl.BlockSpec((tm, nc), lambda j, t, te, nu: (t, j)),
            scratch_shapes=[pltpu.VMEM((d, nc), BF16), pltpu.VMEM((d, nc), BF16)]),
        out_shape=jax.ShapeDtypeStruct((n_rows, f), BF16),
        compiler_params=_params(("arbitrary", "arbitrary")),
        name="moe_ffn1",
    )(tile_e, n_used, xs, w_e1, w_e1, b1, b1)


def _ffn2_kernel(te_ref, nu_ref, a_ref, w_ref, b_ref, o_ref, w_s):
    t = pl.program_id(1)
    prev = te_ref[jnp.maximum(t - 1, 0)]
    changed = (t == 0) | (te_ref[t] != prev)

    @pl.when(changed)
    def _():
        w_s[...] = w_ref[...].astype(BF16)

    @pl.when(t < nu_ref[0])
    def _():
        o_ref[...] = _dot(a_ref[...], w_s[...]) + b_ref[...]

    @pl.when(t >= nu_ref[0])
    def _():
        o_ref[...] = jnp.zeros(o_ref.shape, o_ref.dtype)


def _ffn2(a, tile_e, n_used, w_e2, b_e2, layer, tm, nc=2048):
    n_rows, f = a.shape
    d = w_e2.shape[-1]
    nc = _pick(d, nc)
    nj = d // nc
    n_tiles = n_rows // tm
    b2 = b_e2.reshape(b_e2.shape[0], b_e2.shape[1], 1, d)

    def tcl(t, nu):
        return jnp.minimum(t, nu[0] - 1)

    return pl.pallas_call(
        _ffn2_kernel,
        grid_spec=pltpu.PrefetchScalarGridSpec(
            num_scalar_prefetch=2,
            grid=(nj, n_tiles),
            in_specs=[pl.BlockSpec((tm, f), lambda j, t, te, nu: (tcl(t, nu), 0)),
                      pl.BlockSpec((None, None, f, nc), lambda j, t, te, nu: (layer, te[t], 0, j)),
                      pl.BlockSpec((None, None, 1, nc), lambda j, t, te, nu: (layer, te[t], 0, j))],
            out_specs=pl.BlockSpec((tm, nc), lambda j, t, te, nu: (t, j)),
            scratch_shapes=[pltpu.VMEM((f, nc), BF16)]),
        out_shape=jax.ShapeDtypeStruct((n_rows, d), F32),
        compiler_params=_params(("arbitrary", "arbitrary")),
        name="moe_ffn2",
    )(tile_e, n_used, a, w_e2, b2)


def _combine_kernel(slot_ref, slot_next_ref, x_ref, w_ref, o_hbm, y_ref, buf, sem, *, tm, nt):
    i = pl.program_id(0)

    def row_copy(sref, b, r, kk):
        return pltpu.make_async_copy(o_hbm.at[pl.ds(sref[r * TOP_K + kk], 1)],
                                     buf.at[b, kk, pl.ds(r, 1)], sem.at[b])

    def issue(sref, b):
        def body(r, c):
            for kk in range(TOP_K):
                row_copy(sref, b, r, kk).start()
            return c
        lax.fori_loop(0, tm, body, 0)

    def drain(sref, b):
        def body(r, c):
            for kk in range(TOP_K):
                row_copy(sref, b, r, kk).wait()
            return c
        lax.fori_loop(0, tm, body, 0)

    cur = i % 2

    @pl.when(i == 0)
    def _():
        issue(slot_ref, 0)

    @pl.when(i + 1 < nt)
    def _():
        issue(slot_next_ref, 1 - cur)

    drain(slot_ref, cur)
    w = w_ref[...]
    acc = w[:, 0:1] * buf[cur, 0]
    for kk in range(1, TOP_K):
        acc = acc + w[:, kk:kk + 1] * buf[cur, kk]
    y_ref[...] = x_ref[...] + acc


def _combine(x, wts, slots, o, tm=128):
    t, d = x.shape
    tm = _pick(t, tm)
    nt = t // tm
    kern = functools.partial(_combine_kernel, tm=tm, nt=nt)
    flat = slots.reshape(-1)
    return pl.pallas_call(
        kern,
        grid=(nt,),
        in_specs=[pl.BlockSpec((tm * TOP_K,), lambda i: (i,), memory_space=pltpu.SMEM),
                  pl.BlockSpec((tm * TOP_K,), lambda i: (jnp.minimum(i + 1, nt - 1),), memory_space=pltpu.SMEM),
                  pl.BlockSpec((tm, d), lambda i: (i, 0)),
                  pl.BlockSpec((tm, LANES), lambda i: (i, 0)),
                  pl.BlockSpec(memory_space=pl.ANY)],
        out_specs=pl.BlockSpec((tm, d), lambda i: (i, 0)),
        out_shape=jax.ShapeDtypeStruct((t, d), F32),
        scratch_shapes=[pltpu.VMEM((2, TOP_K, tm, d), F32), pltpu.SemaphoreType.DMA((2,))],
        compiler_params=_params(("arbitrary",)),
        name="moe_combine",
    )(flat, flat, x, wts, o)


def _moe(x, g, w_router, b_router, w_e1, b_e1, w_e2, b_e2, layer):
    t = x.shape[0]
    n_exp = w_router.shape[1]
    tm = MOE_TM
    xp, idxrank, wts, cnt = _router(x, g, w_router, b_router)
    idx = idxrank[:, :TOP_K]
    rank = idxrank[:, TOP_K:2 * TOP_K]
    cnt = cnt[0, :n_exp]
    ptiles = (cnt + tm - 1) // tm
    tend = jnp.cumsum(ptiles)
    pstart = (tend - ptiles) * tm
    onehot = idx[:, :, None] == jnp.arange(n_exp, dtype=I32)[None, None, :]
    slots = (jnp.sum(jnp.where(onehot, pstart[None, None, :], 0), axis=-1) + rank).astype(I32)
    n_tiles = (t * TOP_K + n_exp * (tm - 1)) // tm
    n_used = tend[-1:].astype(I32)
    tid = jnp.minimum(jnp.arange(n_tiles, dtype=I32), n_used[0] - 1)
    tile_e = jnp.minimum(jnp.sum(tid[:, None] >= tend[None, :], axis=1), n_exp - 1).astype(I32)
    xs = _dispatch(xp, slots, n_tiles * tm)
    a = _ffn1(xs, tile_e, n_used, w_e1, b_e1, layer, tm)
    o = _ffn2(a, tile_e, n_used, w_e2, b_e2, layer, tm)
    return _combine(x, wts, slots, o)


def _att_weights(w_in, w_uq):
    d = w_in.shape[0]
    bw = BAND_HEADS * BAND_HDIM
    o1, o2, o3 = Q_RANK, Q_RANK + KV_RANK, Q_RANK + KV_RANK + ROPE_DIM
    half = ROPE_DIM // 2
    kr = w_in[:, o2:o3]
    krs = jnp.concatenate([kr[:, half:], kr[:, :half]], axis=1)
    pad = jnp.zeros((d, 2048 - (o3 + ROPE_DIM)), w_in.dtype)
    w_in2 = jnp.concatenate([w_in[:, :o3], krs, pad, w_in[:, o3:o3 + 3 * bw]], axis=1).astype(BF16)
    w3 = w_uq.reshape(Q_RANK, MLA_HEADS, NOPE_DIM + ROPE_DIM)
    r = w3[:, :, NOPE_DIM:]
    rs = jnp.concatenate([r[:, :, half:], r[:, :, :half]], axis=-1)
    w_uq2 = jnp.concatenate([w3[:, :, :NOPE_DIM].reshape(Q_RANK, -1),
                             jnp.concatenate([r, rs], axis=-1).reshape(Q_RANK, -1)], axis=1).astype(BF16)
    return w_in2, w_uq2


def _att_layer(x, seq, nb, ls, past, cs, c_lat, c_kr, c_bk, c_bv, g_mix, w_in, g_q, w_uq, g_kv, w_ukv,
               rel_bias, w_out):
    ts = nb * ls
    keep = c_bk.shape[1]
    assert past % CHUNK == 0 and ls <= CHUNK and keep == BAND_PAST_CHUNKS * CHUNK and past >= keep
    assert 2 * BAND_TQ == BAND_PAST_CHUNKS * CHUNK and seq % 512 == 0
    w_in2, w_uq2 = _att_weights(w_in, w_uq)
    z = _mm(x, w_in2, gain=g_mix, name="att_in")
    lat, krr, krp = _att_post(z, g_kv, cs)
    q = _mm(z, w_uq2, gain=g_q, xcol=0, name="att_uq")
    w_ukv2 = w_ukv.astype(BF16)
    kv_p = _mm(lat, w_ukv2, rows=seq, out_dtype=BF16, name="att_ukv_p")
    o_a = _mla_prompt(q, cs, kv_p, krp, seq)
    bias_p, bias_s = _band_bias(rel_bias, ls, keep + ls)
    o_b = _band_prompt(z, bias_p, seq)
    lat_s = lat[seq:].reshape(nb, ls, KV_RANK)
    lat_all = jnp.concatenate([c_lat.astype(F32), lat_s], axis=1)
    nk = past + ls
    kv_s = _mm(lat_all.reshape(nb * nk, KV_RANK), w_ukv2, out_dtype=BF16, tm=528, name="att_ukv_s")
    kv_s = kv_s.reshape(nb, nk, -1)
    c_krp = jnp.concatenate([c_kr.astype(F32), jnp.zeros(c_kr.shape[:-1] + (LANES - ROPE_DIM,), F32)], axis=-1)
    kr_all = jnp.concatenate([c_krp.astype(BF16), krp[seq:].reshape(nb, ls, LANES)], axis=1)
    q_s = q[seq:].reshape(nb, ls, -1)
    cs_s = cs[seq:].reshape(nb, ls, LANES)
    o_as = _attn1(q_s, kv_s, kv_s, heads=MLA_HEADS, tq=ls, q0=0, k0=(0, 2), v0=(1, 2),
                  scale=(NOPE_DIM + ROPE_DIM) ** -0.5, qr=q_s, qr0=MLA_HEADS, cs=cs_s, kr=kr_all,
                  name="mla_sample")
    z_s = z[seq:].reshape(nb, ls, -1)
    bw = BAND_HEADS * BAND_HDIM
    kb_all = jnp.concatenate([c_bk.reshape(nb, keep, bw).astype(F32), z_s[:, :, 4096:4096 + bw]], axis=1)
    vb_all = jnp.concatenate([c_bv.reshape(nb, keep, bw).astype(F32), z_s[:, :, 6144:6144 + bw]], axis=1)
    o_bs = _attn1(z_s, kb_all, vb_all, heads=BAND_HEADS, tq=ls, q0=2048 // LANES, k0=(0, 1), v0=(0, 1),
                  scale=BAND_HDIM ** -0.5, bias=bias_s, name="band_sample")
    o = jnp.concatenate([jnp.concatenate([o_a, o_b], axis=1),
                         jnp.concatenate([o_as.reshape(ts, -1), o_bs.reshape(ts, -1)], axis=1)], axis=0)
    x = _mm(o, w_out.astype(BF16), residual=x, name="att_out")
    outs = dict(
        lat_p=lat[:seq], kr_p=krr[:seq, :ROPE_DIM],
        bk_p=z[seq - min(keep, seq):seq, 4096:4096 + bw], bv_p=z[seq - min(keep, seq):seq, 6144:6144 + bw],
        lat_s=lat_s, kr_s=krr[seq:, :ROPE_DIM].reshape(nb, ls, ROPE_DIM),
        bk_s=z_s[:, :, 4096:4096 + bw], bv_s=z_s[:, :, 6144:6144 + bw])
    return x, outs


def _rec_layer(x, seq, nb, ls, s5r, s5i, hg0, lb, g_mix, w_in, lam_re, lam_im, log_dt, b_re, b_im, c_re, c_im,
               d_skip, w_glu, g_hg, w_out):
    z = _mm(x, w_in.astype(BF16), gain=g_mix, name="rec_in")
    prep = _s5_prep(lam_re, lam_im, log_dt, b_re, b_im, c_re, c_im)
    ns = prep[0].shape[0]
    g, p = lam_re.shape
    zeros_p = jnp.zeros((1, ns, SUBLANES, prep[0].shape[2] // 2), F32)
    y_p, hr_p, hi_p = _s5(z, prep, d_skip, zeros_p, zeros_p, batch=1, length=seq, row0=0, tm=_pick(seq, 512))
    y_s, hr_s, hi_s = _s5(z, prep, d_skip, _s5_state_in(s5r, ns), _s5_state_in(s5i, ns),
                          batch=nb, length=ls, row0=seq, tm=ls)
    y = jnp.concatenate([y_p, y_s], axis=0)
    o_c = _mm(y, w_glu.astype(BF16), glu=y, out_dtype=BF16, name="rec_glu")
    hz = jnp.zeros((1, HG_HEADS, HG_VDIM, HG_KDIM), F32)
    od_p, sf_p = _hgrn(z, lb, g_hg, hz, batch=1, length=seq, row0=0, c=_pick(seq, 256))
    od_s, sf_s = _hgrn(z, lb, g_hg, jnp.swapaxes(hg0.astype(F32), -1, -2), batch=nb, length=ls, row0=seq, c=ls)
    o = jnp.concatenate([o_c, jnp.concatenate([od_p, od_s], axis=0)], axis=1)
    x = _mm(o, w_out.astype(BF16), residual=x, name="rec_out")
    outs = dict(sr_p=hr_p.reshape(1, g, p), si_p=hi_p.reshape(1, g, p), hg_p=jnp.swapaxes(sf_p, -1, -2),
                sr_s=hr_s.reshape(nb, g, p), si_s=hi_s.reshape(nb, g, p), hg_s=jnp.swapaxes(sf_s, -1, -2))
    return x, outs


def _mem_layer(x, seq, nb, ls, mem, cmk, cmv, g_x, g_m, w_mq, w_mkv, w_mo):
    n_mem = mem.shape[0]
    mw = MEM_HEADS * MEM_HDIM
    kvm = _mm(mem, w_mkv.astype(BF16), gain=g_m, tm=n_mem, name="mem_kv")
    q = _mm(x, w_mq.astype(BF16), gain=g_x, out_dtype=BF16, name="mem_q")
    scale = MEM_HDIM ** -0.5
    kvm3 = kvm.reshape(1, n_mem, 2 * mw)
    o_p = _attn1(q[:seq].reshape(1, seq, mw), kvm3, kvm3, heads=MEM_HEADS, tq=_pick(seq, 512), q0=0,
                 k0=(0, 1), v0=(MEM_HEADS, 1), scale=scale, name="mem_attn_p")
    o_s = _attn1(q[seq:].reshape(nb, ls, mw), cmk.reshape(nb, n_mem, mw), cmv.reshape(nb, n_mem, mw),
                 heads=MEM_HEADS, tq=ls, q0=0, k0=(0, 1), v0=(0, 1), scale=scale, name="mem_attn_s")
    o = jnp.concatenate([o_p.reshape(seq, mw), o_s.reshape(nb * ls, mw)], axis=0)
    x = _mm(o, w_mo.astype(BF16), residual=x, name="mem_out")
    mk = kvm[:, :mw].reshape(1, n_mem, MEM_HEADS, MEM_HDIM)
    mv = kvm[:, mw:].reshape(1, n_mem, MEM_HEADS, MEM_HDIM)
    return x, mk, mv


def kernel(x_prompt, x_sample, cache_mla_latent, cache_mla_krope, cache_band_k, cache_band_v, cache_mem_k,
           cache_mem_v, state_s5_re, state_s5_im, state_hgrn, mem_prompt, g_mix, g_xattn, g_mem, g_moe,
           g_final, w_in_att, g_q, w_uq, g_kv, w_ukv, rel_bias, w_out_att, w_in_rec, s5_lam_re, s5_lam_im,
           s5_log_dt, s5_b_re, s5_b_im, s5_c_re, s5_c_im, s5_d, w_glu, hg_lb_logits, g_hg, w_out_rec, w_mq,
           w_mkv, w_mo, w_router, b_router, w_e1, b_e1, w_e2, b_e2):
    bp, seq, d = x_prompt.shape
    nb, ls, _ = x_sample.shape
    assert bp == 1
    past = cache_mla_latent.shape[2]
    depth = g_mix.shape[0]
    x = jnp.concatenate([x_prompt.reshape(seq, d), x_sample.reshape(nb * ls, d)], axis=0).astype(F32)
    pos = jnp.concatenate([jnp.arange(seq), jnp.tile(past + jnp.arange(ls), nb)])
    cs = _rope_table(pos)
    probs = jax.nn.softmax(hg_lb_logits.astype(F32), axis=0)
    lb_all = jnp.cumsum(probs, axis=0) - probs
    att, rec, mks, mvs = [], [], [], []
    for l in range(depth):
        j = l // 2
        if l % 2 == 0:
            x, o = _att_layer(x, seq, nb, ls, past, cs, cache_mla_latent[j], cache_mla_krope[j],
                              cache_band_k[j], cache_band_v[j], g_mix[l], w_in_att[j], g_q[j], w_uq[j],
                              g_kv[j], w_ukv[j], rel_bias[j], w_out_att[j])
            att.append(o)
        else:
            x, o = _rec_layer(x, seq, nb, ls, state_s5_re[j], state_s5_im[j], state_hgrn[j], lb_all[j],
                              g_mix[l], w_in_rec[j], s5_lam_re[j], s5_lam_im[j], s5_log_dt[j], s5_b_re[j],
                              s5_b_im[j], s5_c_re[j], s5_c_im[j], s5_d[j], w_glu[j], g_hg[j], w_out_rec[j])
            rec.append(o)
        x, mk, mv = _mem_layer(x, seq, nb, ls, mem_prompt[0], cache_mem_k[l], cache_mem_v[l], g_xattn[l],
                               g_mem[l], w_mq[l], w_mkv[l], w_mo[l])
        mks.append(mk)
        mvs.append(mv)
        x = _moe(x, g_moe[l], w_router[l], b_router[l], w_e1, b_e1, w_e2, b_e2, l)
    y = _rmsnorm_rows(x, g_final)

    def st(lst, key, shape):
        return jnp.stack([o[key].reshape(shape) for o in lst], axis=0)

    keep_p = min(BAND_PAST_CHUNKS * CHUNK, seq)
    bshape_p = (1, keep_p, BAND_HEADS, BAND_HDIM)
    bshape_s = (nb, ls, BAND_HEADS, BAND_HDIM)
    gsz, psz = s5_lam_re.shape[1:]
    hshape = (HG_HEADS, HG_KDIM, HG_VDIM)
    return (y[:seq].reshape(1, seq, d), y[seq:].reshape(nb, ls, d),
            st(att, "lat_p", (1, seq, KV_RANK)), st(att, "kr_p", (1, seq, ROPE_DIM)),
            st(att, "bk_p", bshape_p), st(att, "bv_p", bshape_p),
            jnp.stack(mks, 0), jnp.stack(mvs, 0),
            st(rec, "sr_p", (1, gsz, psz)), st(rec, "si_p", (1, gsz, psz)), st(rec, "hg_p", (1,) + hshape),
            st(att, "lat_s", (nb, ls, KV_RANK)), st(att, "kr_s", (nb, ls, ROPE_DIM)),
            st(att, "bk_s", bshape_s), st(att, "bv_s", bshape_s),
            st(rec, "sr_s", (nb, gsz, psz)), st(rec, "si_s", (nb, gsz, psz)), st(rec, "hg_s", (nb,) + hshape))
```
